```python
import jax, jax.numpy as jnp
from jax import lax
import numpy as np

D_MODEL = 1024
BATCH = 4
SEQ = 4096
DEPTH = 4

N_MIXERS = 3
N_LAYERS_A = (DEPTH + 2) // 3
N_LAYERS_B = (DEPTH + 1) // 3
N_LAYERS_C = DEPTH // 3
RMS_EPS = 1e-6

LRU_WIDTH = D_MODEL
LRU_HEADS = 4
LRU_BLOCK = LRU_WIDTH // LRU_HEADS
LRU_CONV = 4
LRU_C = 8.0

ATT_HEAD_DIM = 64
ATT_HEADS = D_MODEL // ATT_HEAD_DIM
DILATED_GROUPS = ((128, 1), (512, 4), (2048, 16))
N_GROUPS = 3
BAND = 128
ROPE_THETA = 10000.0
NEG_INF = -1e30

RWKV_HEAD = 64
RWKV_HEADS = D_MODEL // RWKV_HEAD
DECAY_LORA = 64
AAA_LORA = 64
GATE_LORA = 128
GN_EPS = 64e-5

FFN_DIM = 2816
FFN_CONV = 3
PLE_DIM = 256

kernel_name = "hybrid_rglru_dilated_rwkv7_trunk"


def rmsnorm(x, g):
    xf = x.astype(jnp.float32)
    y = xf * lax.rsqrt(jnp.mean(xf * xf, axis=-1, keepdims=True) + RMS_EPS)
    return (y * g.astype(jnp.float32)).astype(x.dtype)


def causal_dwconv(x, w, b):
    K, C = w.shape
    y = lax.conv_general_dilated(
        x, w[:, None, :].astype(x.dtype), window_strides=(1,), padding=((K - 1, 0),),
        dimension_numbers=('NWC', 'WIO', 'NWC'), feature_group_count=C)
    return y + b.astype(x.dtype)


def rotary(x, cos, sin):
    x1, x2 = jnp.split(x.astype(jnp.float32), 2, axis=-1)
    return jnp.concatenate([x1 * cos - x2 * sin, x2 * cos + x1 * sin], axis=-1).astype(x.dtype)


def rglru_mixer(x, w_in, conv_w, conv_b, gate_w, gate_b, lam, w_out):
    B, S, _ = x.shape
    f32 = jnp.float32
    y_gate, xr = jnp.split(x @ w_in, 2, axis=-1)
    y_gate = jax.nn.gelu(y_gate)
    xr = causal_dwconv(xr, conv_w, conv_b)
    xh = xr.astype(f32).reshape(B, S, LRU_HEADS, LRU_BLOCK)
    gates = jnp.einsum('bshi,hio->bsho', xh, gate_w.astype(f32)) + gate_b.astype(f32)
    r = jax.nn.sigmoid(gates[..., :LRU_BLOCK]).reshape(B, S, LRU_WIDTH)
    i = jax.nn.sigmoid(gates[..., LRU_BLOCK:]).reshape(B, S, LRU_WIDTH)
    log_a = -LRU_C * r * jax.nn.softplus(-lam.astype(f32))
    a = jnp.exp(log_a)
    bterm = jnp.sqrt(-jnp.expm1(2.0 * log_a)) * (i * xh.reshape(B, S, LRU_WIDTH))

    def combine(c1, c2):
        a1, b1 = c1
        a2, b2 = c2
        return a1 * a2, a2 * b1 + b2

    _, h = lax.associative_scan(combine, (a, bterm), axis=1)
    return (h.astype(x.dtype) * y_gate) @ w_out


def dilated_window_attention(q, k, v, window, dilation):
    B, S, H, Dh = q.shape
    f32 = jnp.float32
    L = S // dilation
    reach = window // dilation
    nblk = -(-L // BAND)
    Lp = nblk * BAND

    def strided(t):
        return t.reshape(B, L, dilation, H, Dh).transpose(0, 2, 1, 3, 4)

    qs = jnp.pad(strided(q), ((0, 0), (0, 0), (0, Lp - L), (0, 0), (0, 0)))
    ks = jnp.pad(strided(k), ((0, 0), (0, 0), (BAND, Lp - L), (0, 0), (0, 0)))
    vs = jnp.pad(strided(v), ((0, 0), (0, 0), (BAND, Lp - L), (0, 0), (0, 0)))
    scale = Dh ** -0.5
    qpos = jnp.arange(BAND)
    kpos = jnp.arange(2 * BAND)
    dist = BAND + qpos[:, None] - kpos[None, :]
    band = (dist >= 0) & (dist <= reach)

    def block(n):
        s0 = n * BAND
        qb = lax.dynamic_slice_in_dim(qs, s0, BAND, axis=2)
        kb = lax.dynamic_slice_in_dim(ks, s0, 2 * BAND, axis=2)
        vb = lax.dynamic_slice_in_dim(vs, s0, 2 * BAND, axis=2)
        logits = jnp.einsum('brqhd,brkhd->brhqk', qb, kb, preferred_element_type=f32) * scale
        valid = band & ((s0 - BAND + kpos) >= 0)[None, :]
        logits = jnp.where(valid, logits, NEG_INF)
        lse = jax.nn.logsumexp(logits, axis=-1)
        probs = jnp.exp(logits - lse[..., None])
        o = jnp.einsum('brhqk,brkhd->brqhd', probs, vb.astype(f32))
        return o, lse

    o, lse = lax.map(block, jnp.arange(nblk))
    o = o.transpose(1, 2, 0, 3, 4, 5).reshape(B, dilation, Lp, H, Dh)[:, :, :L]
    o = o.transpose(0, 2, 1, 3, 4).reshape(B, S, H, Dh)
    lse = lse.transpose(1, 2, 0, 4, 3).reshape(B, dilation, Lp, H)[:, :, :L]
    lse = lse.transpose(0, 2, 1, 3).reshape(B, S, H)
    return o, lse


def dilated_attention_mixer(x, cos, sin, w_qkv, w_out):
    B, S, D = x.shape
    H, Dh = ATT_HEADS, ATT_HEAD_DIM
    qk_cols = N_GROUPS * H * Dh
    qkv = x @ w_qkv
    q = qkv[..., :qk_cols].reshape(B, S, N_GROUPS, H, Dh)
    k = qkv[..., qk_cols:2 * qk_cols].reshape(B, S, N_GROUPS, H, Dh)
    v = qkv[..., 2 * qk_cols:].reshape(B, S, H, Dh)
    c = cos[:, :, None, None, :]
    s = sin[:, :, None, None, :]
    q = rotary(q, c, s)
    k = rotary(k, c, s)
    outs, lses = [], []
    for g, (window, dilation) in enumerate(DILATED_GROUPS):
        o_g, lse_g = dilated_window_attention(q[:, :, g], k[:, :, g], v, window, dilation)
        outs.append(o_g)
        lses.append(lse_g)
    wts = jax.nn.softmax(jnp.stack(lses, axis=0), axis=0)
    o = jnp.einsum('gbsh,gbshd->bshd', wts, jnp.stack(outs, axis=0))
    return o.reshape(B, S, H * Dh).astype(x.dtype) @ w_out


def rwkv7_mixer(x, mu, w_rkv, w0, w1, w2, a0, a1, a2, g1, g2, k_k, k_a, r_k, ln_w, ln_b, w_out):
    B, S, D = x.shape
    H, N = RWKV_HEADS, RWKV_HEAD
    f32 = jnp.float32
    xx = jnp.pad(x, ((0, 0), (1, 0), (0, 0)))[:, :-1] - x
    xmix = x[None] + xx[None] * mu[:, None, None, :].astype(x.dtype)
    rkv = jnp.einsum('cbsd,cde->cbse', xmix[:3], w_rkv)
    r, k, v = rkv[0], rkv[1], rkv[2]
    xw, xa, xg = xmix[3], xmix[4], xmix[5]
    w = -jax.nn.softplus(-(w0 + jnp.tanh(xw @ w1) @ w2).astype(f32)) - 0.5
    decay = jnp.exp(-jnp.exp(w)).reshape(B, S, H, N)
    a = jax.nn.sigmoid((a0 + (xa @ a1) @ a2).astype(f32))
    g = jax.nn.sigmoid(xg @ g1) @ g2
    kk = (k.astype(f32) * k_k.astype(f32)).reshape(B, S, H, N)
    kk = kk / jnp.maximum(jnp.sqrt(jnp.sum(kk * kk, axis=-1, keepdims=True)), 1e-12)
    a_h = a.reshape(B, S, H, N)
    k_h = (k.astype(f32) * (1.0 + (a - 1.0) * k_a.astype(f32))).reshape(B, S, H, N)
    r_h = r.astype(f32).reshape(B, S, H, N)
    v_h = v.astype(f32).reshape(B, S, H, N)
    st_a = -kk
    st_b = kk * a_h

    def step(state, inp):
        r_t, w_t, k_t, v_t, a_t, b_t = inp
        sa = jnp.einsum('bhij,bhj->bhi', state, a_t)
        state = state * w_t[:, :, None, :] + sa[..., None] * b_t[:, :, None, :] + v_t[..., None] * k_t[:, :, None, :]
        y = jnp.einsum('bhij,bhj->bhi', state, r_t)
        return state, y

    tm = lambda t: jnp.swapaxes(t, 0, 1)
    xs = (tm(r_h), tm(decay), tm(k_h), tm(v_h), tm(st_a), tm(st_b))
    _, y = lax.scan(step, jnp.zeros((B, H, N, N), f32), xs)
    y = tm(y)
    mean = jnp.mean(y, axis=-1, keepdims=True)
    var = jnp.mean(jnp.square(y - mean), axis=-1, keepdims=True)
    y = ((y - mean) * lax.rsqrt(var + GN_EPS)).reshape(B, S, D) * ln_w.astype(f32) + ln_b.astype(f32)
    bonus = jnp.sum(r_h * k_h * r_k.astype(f32), axis=-1, keepdims=True) * v_h
    y = y + bonus.reshape(B, S, D)
    return (y.astype(x.dtype) * g) @ w_out


def conv_glu_ffn(x, w_up, conv_w, conv_b, w_down):
    u = causal_dwconv(x @ w_up, conv_w, conv_b)
    gate, up = jnp.split(u, 2, axis=-1)
    return (jax.nn.silu(gate) * up) @ w_down


def setup_inputs(seed: int = 0) -> dict:
    key = jax.random.key(seed)
    ks = iter(jax.random.split(key, 64))
    f32 = jnp.float32
    D = D_MODEL

    def nrm(shape, scale):
        return jax.random.normal(next(ks), shape, f32) * scale

    def gain(shape):
        return 1.0 + nrm(shape, 0.05)

    x = nrm((BATCH, SEQ, D), 1.0)
    p = nrm((DEPTH, BATCH, SEQ, PLE_DIM), 1.0)
    offset = jax.random.randint(next(ks), (BATCH, 1), 0, 1024, dtype=jnp.int32)
    positions = offset + jnp.arange(SEQ, dtype=jnp.int32)[None, :]

    u = jax.random.uniform(next(ks), (N_LAYERS_A, LRU_WIDTH), f32, 0.9, 0.999)
    a_base = u ** (1.0 / LRU_C)
    return {
        "x": x, "p": p, "positions": positions,
        "norm_mix": gain((DEPTH, D)), "norm_ffn": gain((DEPTH, D)), "norm_ple": gain((DEPTH, D)),
        "norm_final": gain((D,)),
        "a_w_in": nrm((N_LAYERS_A, D, 2 * LRU_WIDTH), D ** -0.5),
        "a_conv_w": nrm((N_LAYERS_A, LRU_CONV, LRU_WIDTH), LRU_CONV ** -0.5),
        "a_conv_b": nrm((N_LAYERS_A, LRU_WIDTH), 0.01),
        "a_gate_w": nrm((N_LAYERS_A, LRU_HEADS, LRU_BLOCK, 2 * LRU_BLOCK), LRU_BLOCK ** -0.5),
        "a_gate_b": nrm((N_LAYERS_A, LRU_HEADS, 2 * LRU_BLOCK), 0.01),
        "a_lambda": jnp.log(a_base) - jnp.log1p(-a_base),
        "a_w_out": nrm((N_LAYERS_A, LRU_WIDTH, D), LRU_WIDTH ** -0.5),
        "b_w_qkv": nrm((N_LAYERS_B, D, (2 * N_GROUPS + 1) * ATT_HEADS * ATT_HEAD_DIM), D ** -0.5),
        "b_w_out": nrm((N_LAYERS_B, ATT_HEADS * ATT_HEAD_DIM, D), D ** -0.5),
        "c_mu": jax.random.uniform(next(ks), (N_LAYERS_C, 6, D), f32, 0.0, 1.0),
        "c_w_rkv": nrm((N_LAYERS_C, 3, D, D), D ** -0.5),
        "c_w0": jax.random.uniform(next(ks), (N_LAYERS_C, D), f32, -6.0, -1.0),
        "c_w1": nrm((N_LAYERS_C, D, DECAY_LORA), 0.1 * D ** -0.5),
        "c_w2": nrm((N_LAYERS_C, DECAY_LORA, D), 0.1 * DECAY_LORA ** -0.5),
        "c_a0": nrm((N_LAYERS_C, D), 0.1),
        "c_a1": nrm((N_LAYERS_C, D, AAA_LORA), 0.1 * D ** -0.5),
        "c_a2": nrm((N_LAYERS_C, AAA_LORA, D), 0.1 * AAA_LORA ** -0.5),
        "c_g1": nrm((N_LAYERS_C, D, GATE_LORA), D ** -0.5),
        "c_g2": nrm((N_LAYERS_C, GATE_LORA, D), GATE_LORA ** -0.5),
        "c_k_k": 0.85 + nrm((N_LAYERS_C, D), 0.05),
        "c_k_a": gain((N_LAYERS_C, D)),
        "c_r_k": nrm((N_LAYERS_C, RWKV_HEADS, RWKV_HEAD), 0.1),
        "c_ln_w": gain((N_LAYERS_C, D)),
        "c_ln_b": nrm((N_LAYERS_C, D), 0.01),
        "c_w_out": nrm((N_LAYERS_C, D, D), D ** -0.5),
        "f_w_up": nrm((DEPTH, D, 2 * FFN_DIM), D ** -0.5),
        "f_conv_w": nrm((DEPTH, FFN_CONV, 2 * FFN_DIM), FFN_CONV ** -0.5),
        "f_conv_b": nrm((DEPTH, 2 * FFN_DIM), 0.01),
        "f_w_down": nrm((DEPTH, FFN_DIM, D), FFN_DIM ** -0.5),
        "ple_w_proj": nrm((DEPTH, PLE_DIM, D), PLE_DIM ** -0.5),
        "ple_w_gate": nrm((DEPTH, D, D), D ** -0.5),
    }


def reference(x, p, positions, norm_mix, norm_ffn, norm_ple, norm_final,
              a_w_in, a_conv_w, a_conv_b, a_gate_w, a_gate_b, a_lambda, a_w_out,
              b_w_qkv, b_w_out,
              c_mu, c_w_rkv, c_w0, c_w1, c_w2, c_a0, c_a1, c_a2, c_g1, c_g2,
              c_k_k, c_k_a, c_r_k, c_ln_w, c_ln_b, c_w_out,
              f_w_up, f_conv_w, f_conv_b, f_w_down, ple_w_proj, ple_w_gate):
    inv_freq = ROPE_THETA ** (-jnp.arange(0, ATT_HEAD_DIM, 2, dtype=jnp.float32) / ATT_HEAD_DIM)
    ang = positions.astype(jnp.float32)[..., None] * inv_freq
    cos, sin = jnp.cos(ang), jnp.sin(ang)

    h = x
    for i in range(DEPTH):
        kind, j = i % N_MIXERS, i // N_MIXERS
        hn = rmsnorm(h, norm_mix[i])
        if kind == 0:
            m = rglru_mixer(hn, a_w_in[j], a_conv_w[j], a_conv_b[j], a_gate_w[j], a_gate_b[j],
                            a_lambda[j], a_w_out[j])
        elif kind == 1:
            m = dilated_attention_mixer(hn, cos, sin, b_w_qkv[j], b_w_out[j])
        else:
            m = rwkv7_mixer(hn, c_mu[j], c_w_rkv[j], c_w0[j], c_w1[j], c_w2[j], c_a0[j], c_a1[j],
                            c_a2[j], c_g1[j], c_g2[j], c_k_k[j], c_k_a[j], c_r_k[j], c_ln_w[j],
                            c_ln_b[j], c_w_out[j])
        h = h + m
        h = h + conv_glu_ffn(rmsnorm(h, norm_ffn[i]), f_w_up[i], f_conv_w[i], f_conv_b[i], f_w_down[i])
        ple_gate = jax.nn.sigmoid(rmsnorm(h, norm_ple[i]) @ ple_w_gate[i])
        h = h + ple_gate * (p[i].astype(h.dtype) @ ple_w_proj[i])
    return rmsnorm(h, norm_final)
```

```python
import functools
import math

import jax
import jax.numpy as jnp
from jax import lax
from jax.experimental import pallas as pl
from jax.experimental.pallas import tpu as pltpu

F32 = jnp.float32
BF16 = jnp.bfloat16

RMS_EPS = 1e-6
LRU_HEADS = 4
LRU_CONV = 4
LRU_C = 8.0
ATT_HEAD_DIM = 64
DILATIONS = (1, 4, 16)
BAND = 128
ROPE_THETA = 10000.0
NEG_INF = -1e30
HEAD = 64
GN_EPS = 64e-5
CHUNK = 64

LANES = 128
HALO_BF16 = 16
HALO_F32 = 8
VMEM_LIMIT = 56 * 1024 * 1024


def _params(*sem):
    return pltpu.CompilerParams(dimension_semantics=sem, vmem_limit_bytes=VMEM_LIMIT)


def _const_spec(shape):
    nd = len(shape)
    return pl.BlockSpec(shape, lambda *_: (0,) * nd, pipeline_mode=pl.Buffered(1))


def _dot(a, b):
    return jnp.dot(a, b, preferred_element_type=F32)


def _dot_nt(a, b, precision=None):
    return lax.dot_general(a, b, (((1,), (1,)), ((), ())), preferred_element_type=F32, precision=precision)


def _dot_tn(a, b):
    return lax.dot_general(a, b, (((0,), (0,)), ((), ())), preferred_element_type=F32)


def _rms(x, g):
    ms = jnp.mean(x * x, axis=-1, keepdims=True)
    return (x * lax.rsqrt(ms + RMS_EPS)) * g


def _softplus(x):
    return jnp.maximum(x, 0.0) + jnp.log1p(jnp.exp(-jnp.abs(x)))


def _gelu_tanh(x):
    c = math.sqrt(2.0 / math.pi)
    return x * (0.5 * (1.0 + jnp.tanh(c * (x + 0.044715 * (x * x * x)))))


def _seg_sum(x):
    lane = lax.broadcasted_iota(jnp.int32, (1, LANES), 1)
    left = lane < HEAD
    outs = []
    for c in range(x.shape[1] // LANES):
        xc = x[:, LANES * c:LANES * (c + 1)]
        sl = jnp.sum(jnp.where(left, xc, 0.0), axis=-1, keepdims=True)
        sr = jnp.sum(jnp.where(left, 0.0, xc), axis=-1, keepdims=True)
        outs.append(jnp.where(left, sl, sr))
    return jnp.concatenate(outs, axis=1)


def _a_in_kernel(h_ref, g_ref, wg_ref, wx_ref, yg_ref, xr_ref):
    hn = _rms(h_ref[...], g_ref[...]).astype(BF16)
    yg_ref[...] = _gelu_tanh(_dot(hn, wg_ref[...])).astype(yg_ref.dtype)
    xr_ref[...] = _dot(hn, wx_ref[...])


def _a_scan_kernel(xr_ref, yg_ref, h_ref, cw_ref, cb_ref, gw_ref, gb_ref, lam_ref, wo_ref,
                   out_ref, cx_ref, ch_ref, *, tt):
    @pl.when(pl.program_id(1) == 0)
    def _():
        cx_ref[...] = jnp.zeros_like(cx_ref)
        ch_ref[...] = jnp.zeros_like(ch_ref)

    width = xr_ref.shape[1]
    blk = width // LRU_HEADS
    x = xr_ref[...]
    xe = jnp.concatenate([cx_ref[...], x], axis=0)
    cw = cw_ref[...]
    y = cb_ref[...] + cw[LRU_CONV - 1:LRU_CONV] * xe
    for k in range(1, LRU_CONV):
        y = y + cw[LRU_CONV - 1 - k:LRU_CONV - k] * pltpu.roll(xe, k, axis=0)
    y = y[HALO_F32:]
    cx_ref[...] = x[tt - HALO_F32:]

    yb = y.astype(BF16)
    rs, ins = [], []
    for hh in range(LRU_HEADS):
        gts = _dot(yb[:, blk * hh:blk * (hh + 1)], gw_ref[hh]) + gb_ref[:, 2 * blk * hh:2 * blk * (hh + 1)]
        rs.append(jax.nn.sigmoid(gts[:, :blk]))
        ins.append(jax.nn.sigmoid(gts[:, blk:]))
    r = jnp.concatenate(rs, axis=1)
    gi = jnp.concatenate(ins, axis=1)

    log_a = (-LRU_C * r) * _softplus(-lam_ref[...])
    a = jnp.exp(log_a)
    b = jnp.sqrt(-jnp.tanh(log_a) * (a * a + 1.0)) * (gi * y)

    row = lax.broadcasted_iota(jnp.int32, (tt, 1), 0)
    s = 1
    while s < tt:
        a_sh = pltpu.roll(a, s, axis=0)
        b_sh = pltpu.roll(b, s, axis=0)
        ok = row >= s
        b = jnp.where(ok, a * b_sh + b, b)
        a = jnp.where(ok, a * a_sh, a)
        s *= 2
    hs = b + a * ch_ref[0:1, :]
    ch_ref[...] = jnp.broadcast_to(hs[tt - 1:tt, :], ch_ref.shape)

    gated = (hs * yg_ref[...].astype(F32)).astype(BF16)
    out_ref[...] = h_ref[...] + _dot(gated, wo_ref[...])


def _rglru_layer(h, norm_g, w_in, conv_w, conv_b, gate_w, gate_b, lam, w_out, *, batch, seq):
    T, D = h.shape
    W = w_in.shape[1] // 2
    tm = 512
    w_in = w_in.astype(BF16)
    yg, xr = pl.pallas_call(
        _a_in_kernel,
        grid=(T // tm,),
        in_specs=[
            pl.BlockSpec((tm, D), lambda i: (i, 0)),
            _const_spec((1, D)),
            _const_spec((D, W)),
            _const_spec((D, W)),
        ],
        out_specs=[pl.BlockSpec((tm, W), lambda i: (i, 0)), pl.BlockSpec((tm, W), lambda i: (i, 0))],
        out_shape=[jax.ShapeDtypeStruct((T, W), BF16), jax.ShapeDtypeStruct((T, W), F32)],
        compiler_params=_params("parallel"),
        name="a_in",
    )(h, norm_g.reshape(1, D), w_in[:, :W], w_in[:, W:])

    tt = 256
    nt = seq // tt
    blk = W // LRU_HEADS
    row_spec = pl.BlockSpec((tt, W), lambda b, t: (b * nt + t, 0))
    return pl.pallas_call(
        functools.partial(_a_scan_kernel, tt=tt),
        grid=(batch, nt),
        in_specs=[
            row_spec, row_spec, pl.BlockSpec((tt, D), lambda b, t: (b * nt + t, 0)),
            _const_spec((LRU_CONV, W)),
            _const_spec((1, W)),
            _const_spec((LRU_HEADS, blk, 2 * blk)),
            _const_spec((1, 2 * W)),
            _const_spec((1, W)),
            _const_spec((W, D)),
        ],
        out_specs=pl.BlockSpec((tt, D), lambda b, t: (b * nt + t, 0)),
        out_shape=jax.ShapeDtypeStruct((T, D), F32),
        scratch_shapes=[pltpu.VMEM((HALO_F32, W), F32), pltpu.VMEM((HALO_F32, W), F32)],
        compiler_params=_params("parallel", "arbitrary"),
        name="a_scan",
    )(xr, yg, h, conv_w, conv_b.reshape(1, W), gate_w.astype(BF16), gate_b.reshape(1, 2 * W),
      lam.reshape(1, W), w_out.astype(BF16))


def _b_qkv_kernel(h_ref, g_ref, w_ref, cos_ref, sin_ref, out_ref, *, n_rot, n_q):
    hn = _rms(h_ref[...], g_ref[...]).astype(BF16)
    cos = cos_ref[...]
    sin = sin_ref[...]
    lane = lax.broadcasted_iota(jnp.int32, (1, LANES), 1)
    first_half = (lane % ATT_HEAD_DIM) < (ATT_HEAD_DIM // 2)
    width = out_ref.shape[1]
    blk = 1024
    for j in range(width // blk):
        z = _dot(hn, w_ref[:, blk * j:blk * (j + 1)])
        if j < n_rot:
            cols = []
            for c in range(blk // LANES):
                zc = z[:, LANES * c:LANES * (c + 1)]
                up = pltpu.roll(zc, LANES - ATT_HEAD_DIM // 2, axis=1)
                dn = pltpu.roll(zc, ATT_HEAD_DIM // 2, axis=1)
                oc = zc * cos + jnp.where(first_half, up, dn) * sin
                if j < n_q:
                    oc = oc * (ATT_HEAD_DIM ** -0.5)
                cols.append(oc)
            z = jnp.concatenate(cols, axis=1)
        out_ref[:, blk * j:blk * (j + 1)] = z.astype(out_ref.dtype)


def _b_attn_kernel(q_ref, kp_ref, kc_ref, vp_ref, vc_ref, o_ref, l_ref):
    n = pl.program_id(2)
    row = lax.broadcasted_iota(jnp.int32, (BAND, 2 * BAND), 0)
    col = lax.broadcasted_iota(jnp.int32, (BAND, 2 * BAND), 1)
    dist = BAND + row - col
    valid = (dist >= 0) & (dist <= BAND) & ((col >= BAND) | (n > 0))
    lane = lax.broadcasted_iota(jnp.int32, (1, LANES), 1)
    left = lane < ATT_HEAD_DIM
    zero = jnp.zeros((), BF16)
    for p in range(q_ref.shape[1] // LANES):
        sl = slice(LANES * p, LANES * (p + 1))
        q = q_ref[:, sl]
        k = jnp.concatenate([kp_ref[:, sl], kc_ref[:, sl]], axis=0)
        v = jnp.concatenate([vp_ref[:, sl], vc_ref[:, sl]], axis=0)
        outs, lses = [], []
        for hh in range(2):
            qm = jnp.where(left if hh == 0 else ~left, q, zero)
            s = jnp.where(valid, _dot_nt(qm, k), NEG_INF)
            m = jnp.max(s, axis=-1, keepdims=True)
            e = jnp.exp(s - m)
            den = jnp.sum(e, axis=-1, keepdims=True)
            outs.append(_dot(e.astype(BF16), v) / den)
            lses.append(m + jnp.log(den))
        o_ref[:, sl] = jnp.where(left, outs[0], outs[1]).astype(o_ref.dtype)
        l_ref[:, sl] = jnp.where(left, lses[0], lses[1])


def _b_out_kernel(o1_ref, o2_ref, o3_ref, l1_ref, l2_ref, l3_ref, h_ref, wo_ref, out_ref):
    l1, l2, l3 = l1_ref[...], l2_ref[...], l3_ref[...]
    m = jnp.maximum(jnp.maximum(l1, l2), l3)
    e1, e2, e3 = jnp.exp(l1 - m), jnp.exp(l2 - m), jnp.exp(l3 - m)
    o = (e1 * o1_ref[...].astype(F32) + e2 * o2_ref[...].astype(F32) + e3 * o3_ref[...].astype(F32)) / (e1 + e2 + e3)
    out_ref[...] = h_ref[...] + _dot(o.astype(BF16), wo_ref[...])


def _attention_layer(h, norm_g, cos_t, sin_t, w_qkv, w_out, *, batch, seq):
    T, D = h.shape
    HD = w_out.shape[0]
    NQ = w_qkv.shape[1]
    G = len(DILATIONS)
    tm = 512
    qkv = pl.pallas_call(
        functools.partial(_b_qkv_kernel, n_rot=2 * G, n_q=G),
        grid=(T // tm,),
        in_specs=[
            pl.BlockSpec((tm, D), lambda i: (i, 0)),
            _const_spec((1, D)),
            _const_spec((D, NQ)),
            pl.BlockSpec((tm, LANES), lambda i: (i, 0)),
            pl.BlockSpec((tm, LANES), lambda i: (i, 0)),
        ],
        out_specs=pl.BlockSpec((tm, NQ), lambda i: (i, 0)),
        out_shape=jax.ShapeDtypeStruct((T, NQ), BF16),
        compiler_params=_params("parallel"),
        name="b_qkv",
    )(h, norm_g.reshape(1, D), w_qkv.astype(BF16), cos_t, sin_t)

    nblk_cols = NQ // HD
    outs, lses = [], []
    for g, d in enumerate(DILATIONS):
        L = seq // d
        view = qkv.reshape(batch, L, d * NQ)
        blk = (None, BAND, HD)

        def col(off):
            return lambda b, r, n: (b, n, r * nblk_cols + off)

        def col_prev(off):
            return lambda b, r, n: (b, jnp.maximum(n - 1, 0), r * nblk_cols + off)

        o_g, l_g = pl.pallas_call(
            _b_attn_kernel,
            grid=(batch, d, L // BAND),
            in_specs=[
                pl.BlockSpec(blk, col(g)),
                pl.BlockSpec(blk, col_prev(G + g)),
                pl.BlockSpec(blk, col(G + g)),
                pl.BlockSpec(blk, col_prev(2 * G)),
                pl.BlockSpec(blk, col(2 * G)),
            ],
            out_specs=[pl.BlockSpec(blk, lambda b, r, n: (b, n, r)), pl.BlockSpec(blk, lambda b, r, n: (b, n, r))],
            out_shape=[jax.ShapeDtypeStruct((batch, L, d * HD), BF16), jax.ShapeDtypeStruct((batch, L, d * HD), F32)],
            compiler_params=_params("parallel", "parallel", "arbitrary"),
            name=f"b_attn_d{d}",
        )(view, view, view, view, view)
        outs.append(o_g.reshape(T, HD))
        lses.append(l_g.reshape(T, HD))

    row = lambda i: (i, 0)
    return pl.pallas_call(
        _b_out_kernel,
        grid=(T // tm,),
        in_specs=[pl.BlockSpec((tm, HD), row)] * 6 + [pl.BlockSpec((tm, D), row), _const_spec((HD, D))],
        out_specs=pl.BlockSpec((tm, D), row),
        out_shape=jax.ShapeDtypeStruct((T, D), F32),
        compiler_params=_params("parallel"),
        name="b_out",
    )(*outs, *lses, h, w_out.astype(BF16))


def _c_in_kernel(h_ref, hh_ref, g_ref, mu_ref, wr_ref, wk_ref, wv_ref, w1_ref, w2_ref, a1_ref, a2_ref,
                 g1_ref, g2_ref, w0_ref, a0_ref, kk_ref, ka_ref,
                 r_out, lw_out, kh_out, v_out, kn_out, b_out, g_out, *, tiles_per_seq):
    g = g_ref[...]
    hn = _rms(h_ref[...], g)
    halo = _rms(hh_ref[...], g)
    halo = jnp.where(pl.program_id(0) % tiles_per_seq == 0, 0.0, halo)
    he = jnp.concatenate([halo, hn], axis=0)
    xx = pltpu.roll(he, 1, axis=0)[HALO_F32:] - hn
    mu = mu_ref[...]

    def mix(c):
        return (hn + xx * mu[c:c + 1]).astype(BF16)

    r = _dot(mix(0), wr_ref[...])
    k = _dot(mix(1), wk_ref[...])
    v = _dot(mix(2), wv_ref[...])
    wl = _dot(jnp.tanh(_dot(mix(3), w1_ref[...])).astype(BF16), w2_ref[...])
    al = _dot(_dot(mix(4), a1_ref[...]).astype(BF16), a2_ref[...])
    g_out[...] = _dot(jax.nn.sigmoid(_dot(mix(5), g1_ref[...])).astype(BF16), g2_ref[...])

    w = -_softplus(-(w0_ref[...] + wl)) - 0.5
    a = jax.nn.sigmoid(a0_ref[...] + al)
    kk = k * kk_ref[...]
    kk = kk / jnp.maximum(jnp.sqrt(_seg_sum(kk * kk)), 1e-12)
    r_out[...] = r
    lw_out[...] = -jnp.exp(w)
    kh_out[...] = k * (1.0 + (a - 1.0) * ka_ref[...])
    v_out[...] = v
    kn_out[...] = kk
    b_out[...] = kk * a


def _tri_inv(m):
    c = m.shape[0]
    row = lax.broadcasted_iota(jnp.int32, (c, c), 0)
    col = lax.broadcasted_iota(jnp.int32, (c, c), 1)
    hi = lax.Precision.HIGHEST
    x = jnp.where(row == col, 1.0, 0.0) + jnp.where((row >> 1 == col >> 1) & (row > col), m, 0.0)
    k = 1
    while (1 << k) < c:
        off = (row >> (k + 1) == col >> (k + 1)) & (((row >> k) & 1) == 1) & (((col >> k) & 1) == 0)
        mk = jnp.where(off, m, 0.0)
        x = x + jnp.dot(jnp.dot(x, mk, precision=hi, preferred_element_type=F32), x,
                        precision=hi, preferred_element_type=F32)
        k += 1
    return x


def _c_chunk_kernel(r_ref, lw_ref, kh_ref, v_ref, kn_ref, b_ref,
                    w_out, uv_out, rt_out, yv_out, bh_out, prb_out, vk_out, gam_out):
    C = r_ref.shape[0]
    lw = lw_ref[...]
    rowc = lax.broadcasted_iota(jnp.int32, (C, 1), 0)
    cum = lw
    s = 1
    while s < C:
        cum = cum + jnp.where(rowc >= s, pltpu.roll(cum, s, axis=0), 0.0)
        s *= 2
    last = cum[C - 1:C, :]
    e_neg = jnp.exp(-cum)
    e_end = jnp.exp(last - cum)
    a_t = -kn_ref[...] * jnp.exp(cum - lw)
    b_t = b_ref[...] * e_neg
    k_t = kh_ref[...] * e_neg
    r_t = r_ref[...] * jnp.exp(cum)
    k_h = kh_ref[...] * e_end
    v = v_ref[...]
    rt_out[...] = r_t
    bh_out[...] = b_ref[...] * e_end
    gam_out[...] = jnp.broadcast_to(jnp.exp(last), gam_out.shape)

    row = lax.broadcasted_iota(jnp.int32, (C, C), 0)
    col = lax.broadcasted_iota(jnp.int32, (C, C), 1)
    strict = row > col
    incl = row >= col
    lane = lax.broadcasted_iota(jnp.int32, (1, LANES), 1)
    prow = lax.broadcasted_iota(jnp.int32, (LANES, LANES), 0)
    pcol = lax.broadcasted_iota(jnp.int32, (LANES, LANES), 1)
    same_head = (prow < HEAD) == (pcol < HEAD)
    hi = lax.Precision.HIGHEST
    for p in range(r_ref.shape[1] // LANES):
        sl = slice(LANES * p, LANES * (p + 1))
        a_p, b_p, k_p, r_p, v_p = a_t[:, sl], b_t[:, sl], k_t[:, sl], r_t[:, sl], v[:, sl]
        w_acc = uv_acc = yv_acc = None
        for hh in range(2):
            mine = (lane < HEAD) if hh == 0 else (lane >= HEAD)
            a_m = jnp.where(mine, a_p, 0.0)
            r_m = jnp.where(mine, r_p, 0.0)
            v_m = jnp.where(mine, v_p, 0.0)
            m_ab = jnp.where(strict, _dot_nt(a_m, b_p), 0.0)
            m_ak = jnp.where(strict, _dot_nt(a_m, k_p), 0.0)
            p_rb = jnp.where(incl, _dot_nt(r_m, b_p), 0.0)
            p_rk = jnp.where(incl, _dot_nt(r_m, k_p), 0.0)
            t_inv = _tri_inv(m_ab)
            w_h = jnp.dot(t_inv, a_m, precision=hi, preferred_element_type=F32)
            uv_h = jnp.dot(t_inv, _dot(m_ak, v_m), precision=hi, preferred_element_type=F32)
            yv_h = _dot(p_rk, v_m)
            w_acc = w_h if w_acc is None else w_acc + w_h
            uv_acc = uv_h if uv_acc is None else uv_acc + uv_h
            yv_acc = yv_h if yv_acc is None else yv_acc + yv_h
            prb_out[C * (2 * p + hh):C * (2 * p + hh + 1), :] = p_rb
        w_out[:, sl] = w_acc
        uv_out[:, sl] = uv_acc
        yv_out[:, sl] = yv_acc
        vk_out[LANES * p:LANES * (p + 1), :] = jnp.where(same_head, _dot_tn(v_p, k_h[:, sl]), 0.0)


def _c_state_kernel(w_ref, uv_ref, rt_ref, yv_ref, bh_ref, prb_ref, vk_ref, gam_ref, y_out, s_ref):
    @pl.when(pl.program_id(1) == 0)
    def _():
        s_ref[...] = jnp.zeros_like(s_ref)

    C = w_ref.shape[0]
    lane = lax.broadcasted_iota(jnp.int32, (1, LANES), 1)
    left = lane < HEAD
    prow = lax.broadcasted_iota(jnp.int32, (LANES, LANES), 0)
    pcol = lax.broadcasted_iota(jnp.int32, (LANES, LANES), 1)
    same_head = (prow < HEAD) == (pcol < HEAD)
    for p in range(w_ref.shape[1] // LANES):
        sl = slice(LANES * p, LANES * (p + 1))
        st = s_ref[p]
        wr = jnp.concatenate([w_ref[:, sl], rt_ref[:, sl]], axis=0)
        ur = _dot_nt(wr, st)
        u = ur[:C] + uv_ref[:, sl]
        pu0 = _dot(prb_ref[C * (2 * p):C * (2 * p + 1), :], u)
        pu1 = _dot(prb_ref[C * (2 * p + 1):C * (2 * p + 2), :], u)
        y_out[:, sl] = ur[C:] + jnp.where(left, pu0, pu1) + yv_ref[:, sl]
        ds = jnp.where(same_head, _dot_tn(u, bh_ref[:, sl]), 0.0)
        s_ref[p] = st * gam_ref[0:1, sl] + ds + vk_ref[LANES * p:LANES * (p + 1), :]


def _c_out_kernel(y_ref, r_ref, kh_ref, v_ref, g_ref, h_ref, lnw_ref, lnb_ref, rk_ref, wo_ref, out_ref):
    y = y_ref[...]
    mean = _seg_sum(y) * (1.0 / HEAD)
    yc = y - mean
    var = _seg_sum(yc * yc) * (1.0 / HEAD)
    yn = (yc * lax.rsqrt(var + GN_EPS)) * lnw_ref[...] + lnb_ref[...]
    bonus = _seg_sum(r_ref[...] * kh_ref[...] * rk_ref[...]) * v_ref[...]
    out = ((yn + bonus) * g_ref[...]).astype(BF16)
    out_ref[...] = h_ref[...] + _dot(out, wo_ref[...])


def _rwkv_layer(h, norm_g, mu, w_rkv, w0, w1, w2, a0, a1, a2, g1, g2, k_k, k_a, r_k, ln_w, ln_b, w_out,
                *, batch, seq):
    T, D = h.shape
    tm = 256
    row = lambda i: (i, 0)
    vec = lambda x: x.reshape(1, D)
    halo_spec = pl.BlockSpec((HALO_F32, D), lambda i: (jnp.maximum(i * (tm // HALO_F32) - 1, 0), 0))
    w_rkv = w_rkv.astype(BF16)
    consts = [vec(norm_g), mu, w_rkv[0], w_rkv[1], w_rkv[2], w1.astype(BF16), w2.astype(BF16), a1.astype(BF16),
              a2.astype(BF16), g1.astype(BF16), g2.astype(BF16), vec(w0), vec(a0), vec(k_k), vec(k_a)]
    r, lw, kh, v, kn, bb, gg = pl.pallas_call(
        functools.partial(_c_in_kernel, tiles_per_seq=seq // tm),
        grid=(T // tm,),
        in_specs=[pl.BlockSpec((tm, D), row), halo_spec] + [_const_spec(c.shape) for c in consts],
        out_specs=[pl.BlockSpec((tm, D), row)] * 7,
        out_shape=[jax.ShapeDtypeStruct((T, D), F32)] * 7,
        compiler_params=_params("parallel"),
        name="c_in",
    )(h, h, *consts)

    C = CHUNK
    nc = T // C
    npair = D // LANES
    heads = D // HEAD
    crow = pl.BlockSpec((C, D), row)
    wq, uv, rt, yv, bh, prb, vk, gam = pl.pallas_call(
        _c_chunk_kernel,
        grid=(nc,),
        in_specs=[crow] * 6,
        out_specs=[crow] * 5 + [
            pl.BlockSpec((heads * C, C), row),
            pl.BlockSpec((npair * LANES, LANES), row),
            pl.BlockSpec((HALO_F32, D), row),
        ],
        out_shape=[jax.ShapeDtypeStruct((T, D), F32)] * 5 + [
            jax.ShapeDtypeStruct((nc * heads * C, C), F32),
            jax.ShapeDtypeStruct((nc * npair * LANES, LANES), F32),
            jax.ShapeDtypeStruct((nc * HALO_F32, D), F32),
        ],
        compiler_params=_params("parallel"),
        name="c_chunk",
    )(r, lw, kh, v, kn, bb)

    cps = seq // C
    srow = lambda b, c: (b * cps + c, 0)
    y = pl.pallas_call(
        _c_state_kernel,
        grid=(batch, cps),
        in_specs=[pl.BlockSpec((C, D), srow)] * 5 + [
            pl.BlockSpec((heads * C, C), srow),
            pl.BlockSpec((npair * LANES, LANES), srow),
            pl.BlockSpec((HALO_F32, D), srow),
        ],
        out_specs=pl.BlockSpec((C, D), srow),
        out_shape=jax.ShapeDtypeStruct((T, D), F32),
        scratch_shapes=[pltpu.VMEM((npair, LANES, LANES), F32)],
        compiler_params=_params("parallel", "arbitrary"),
        name="c_state",
    )(wq, uv, rt, yv, bh, prb, vk, gam)

    tmo = 512
    return pl.pallas_call(
        _c_out_kernel,
        grid=(T // tmo,),
        in_specs=[pl.BlockSpec((tmo, D), row)] * 6 + [_const_spec((1, D))] * 3 + [_const_spec((D, D))],
        out_specs=pl.BlockSpec((tmo, D), row),
        out_shape=jax.ShapeDtypeStruct((T, D), F32),
        compiler_params=_params("parallel"),
        name="c_out",
    )(y, r, kh, v, gg, h, vec(ln_w), vec(ln_b), vec(r_k), w_out.astype(BF16))


def _ffn_kernel(h_ref, hh_ref, g_ref, wg_ref, wu_ref, cwg_ref, cwu_ref, cbg_ref, cbu_ref, wd_ref,
                out_ref, hn_ref, acc_ref, *, tiles_per_seq):
    g = g_ref[...]
    halo = _rms(hh_ref[...], g)
    halo = jnp.where(pl.program_id(0) % tiles_per_seq == 0, 0.0, halo)
    hn_ref[:HALO_BF16, :] = halo.astype(BF16)
    hn_ref[HALO_BF16:, :] = _rms(h_ref[...], g).astype(BF16)
    acc_ref[...] = jnp.zeros_like(acc_ref)

    def conv(u, cw, cb):
        y = cb + cw[2:3] * u + cw[1:2] * pltpu.roll(u, 1, axis=0) + cw[0:1] * pltpu.roll(u, 2, axis=0)
        return y[HALO_BF16:]

    def chunk(c, carry):
        hn = hn_ref[...]
        yg = conv(_dot(hn, wg_ref[c]), cwg_ref[c], cbg_ref[c])
        yu = conv(_dot(hn, wu_ref[c]), cwu_ref[c], cbu_ref[c])
        act = ((yg * jax.nn.sigmoid(yg)) * yu).astype(BF16)
        acc_ref[...] += _dot(act, wd_ref[c])
        return carry

    lax.fori_loop(0, wg_ref.shape[0], chunk, 0)
    out_ref[...] = h_ref[...] + acc_ref[...]


def _ffn_layer(h, norm_g, w_up, conv_w, conv_b, w_down, *, seq):
    T, D = h.shape
    F = w_down.shape[0]
    fc = 256
    nf = F // fc
    tm = 512
    kc = conv_w.shape[0]
    w_up = w_up.astype(BF16)
    split_w = lambda w: w.reshape(D, nf, fc).transpose(1, 0, 2)
    split_c = lambda c: c.reshape(c.shape[0], nf, fc).transpose(1, 0, 2)
    consts = [norm_g.reshape(1, D), split_w(w_up[:, :F]), split_w(w_up[:, F:]),
              split_c(conv_w[:, :F]), split_c(conv_w[:, F:]),
              split_c(conv_b[None, :F]), split_c(conv_b[None, F:]),
              w_down.astype(BF16).reshape(nf, fc, D)]
    halo_spec = pl.BlockSpec((HALO_BF16, D), lambda i: (jnp.maximum(i * (tm // HALO_BF16) - 1, 0), 0))
    del kc
    return pl.pallas_call(
        functools.partial(_ffn_kernel, tiles_per_seq=seq // tm),
        grid=(T // tm,),
        in_specs=[pl.BlockSpec((tm, D), lambda i: (i, 0)), halo_spec] + [_const_spec(c.shape) for c in consts],
        out_specs=pl.BlockSpec((tm, D), lambda i: (i, 0)),
        out_shape=jax.ShapeDtypeStruct((T, D), F32),
        scratch_shapes=[pltpu.VMEM((tm + HALO_BF16, D), BF16), pltpu.VMEM((tm, D), F32)],
        compiler_params=_params("parallel"),
        name="ffn",
    )(h, h, *consts)


def _ple_kernel(h_ref, p_ref, g_ref, wg_ref, wp_ref, gf_ref, out_ref, *, final_norm):
    h = h_ref[...]
    hn = _rms(h, g_ref[...]).astype(BF16)
    gate = jax.nn.sigmoid(_dot(hn, wg_ref[...]))
    out = h + gate * _dot(p_ref[...].astype(BF16), wp_ref[...])
    if final_norm:
        out = _rms(out, gf_ref[...])
    out_ref[...] = out


def _ple_layer(h, p, layer, norm_g, w_gate, w_proj, norm_final, *, final_norm):
    T, D = h.shape
    P = p.shape[-1]
    tm = 512
    return pl.pallas_call(
        functools.partial(_ple_kernel, final_norm=final_norm),
        grid=(T // tm,),
        in_specs=[
            pl.BlockSpec((tm, D), lambda i: (i, 0)),
            pl.BlockSpec((None, tm, P), lambda i: (layer, i, 0)),
            _const_spec((1, D)),
            _const_spec((D, D)),
            _const_spec((P, D)),
            _const_spec((1, D)),
        ],
        out_specs=pl.BlockSpec((tm, D), lambda i: (i, 0)),
        out_shape=jax.ShapeDtypeStruct((T, D), F32),
        compiler_params=_params("parallel"),
        name="ple",
    )(h, p, norm_g.reshape(1, D), w_gate.astype(BF16), w_proj.astype(BF16), norm_final.reshape(1, D))


def _rotary_tables(positions):
    half = ATT_HEAD_DIM // 2
    inv_freq = ROPE_THETA ** (-jnp.arange(0, ATT_HEAD_DIM, 2, dtype=F32) / ATT_HEAD_DIM)
    ang = positions.astype(F32).reshape(-1, 1) * inv_freq
    cos, sin = jnp.cos(ang), jnp.sin(ang)
    reps = LANES // ATT_HEAD_DIM
    cos_t = jnp.tile(jnp.concatenate([cos, cos], axis=1), (1, reps))
    sin_t = jnp.tile(jnp.concatenate([-sin, sin], axis=1), (1, reps))
    del half
    return cos_t, sin_t


def kernel(x, p, positions, norm_mix, norm_ffn, norm_ple, norm_final, a_w_in, a_conv_w, a_conv_b, a_gate_w, a_gate_b, a_lambda, a_w_out, b_w_qkv, b_w_out, c_mu, c_w_rkv, c_w0, c_w1, c_w2, c_a0, c_a1, c_a2, c_g1, c_g2, c_k_k, c_k_a, c_r_k, c_ln_w, c_ln_b, c_w_out, f_w_up, f_conv_w, f_conv_b, f_w_down, ple_w_proj, ple_w_gate):
    batch, seq, D = x.shape
    depth = norm_mix.shape[0]
    T = batch * seq
    h = x.reshape(T, D)
    pf = p.reshape(depth, T, p.shape[-1])
    cos_t, sin_t = _rotary_tables(positions)
    for i in range(depth):
        kind, j = i % 3, i // 3
        if kind == 0:
            h = _rglru_layer(h, norm_mix[i], a_w_in[j], a_conv_w[j], a_conv_b[j], a_gate_w[j], a_gate_b[j],
                             a_lambda[j], a_w_out[j], batch=batch, seq=seq)
        elif kind == 1:
            h = _attention_layer(h, norm_mix[i], cos_t, sin_t, b_w_qkv[j], b_w_out[j], batch=batch, seq=seq)
        else:
            h = _rwkv_layer(h, norm_mix[i], c_mu[j], c_w_rkv[j], c_w0[j], c_w1[j], c_w2[j], c_a0[j], c_a1[j],
                            c_a2[j], c_g1[j], c_g2[j], c_k_k[j], c_k_a[j], c_r_k[j], c_ln_w[j], c_ln_b[j],
                            c_w_out[j], batch=batch, seq=seq)
        h = _ffn_layer(h, norm_ffn[i], f_w_up[i], f_conv_w[i], f_conv_b[i], f_w_down[i], seq=seq)
        h = _ple_layer(h, pf, i, norm_ple[i], ple_w_gate[i], ple_w_proj[i], norm_final,
                       final_norm=(i == depth - 1))
    return h.reshape(batch, seq, D)
```

```python
import functools
import math

import jax
import jax.numpy as jnp
from jax import lax
from jax.experimental import pallas as pl
from jax.experimental.pallas import tpu as pltpu

F32 = jnp.float32
BF16 = jnp.bfloat16

RMS_EPS = 1e-6
LRU_HEADS = 4
LRU_CONV = 4
LRU_C = 8.0
ATT_HEAD_DIM = 64
DILATIONS = (1, 4, 16)
BAND = 128
ROPE_THETA = 10000.0
NEG_INF = -1e30
HEAD = 64
GN_EPS = 64e-5
CHUNK = 64

LANES = 128
HALO_BF16 = 16
HALO_F32 = 8
VMEM_LIMIT = 56 * 1024 * 1024


def _params(*sem):
    return pltpu.CompilerParams(dimension_semantics=sem, vmem_limit_bytes=VMEM_LIMIT)


def _const_spec(shape):
    nd = len(shape)
    return pl.BlockSpec(shape, lambda *_: (0,) * nd, pipeline_mode=pl.Buffered(1))


def _dot(a, b):
    return jnp.dot(a, b, preferred_element_type=F32)


def _dot_nt(a, b):
    return lax.dot_general(a, b, (((1,), (1,)), ((), ())), preferred_element_type=F32)


def _rms(x, g):
    ms = jnp.mean(x * x, axis=-1, keepdims=True)
    return (x * lax.rsqrt(ms + RMS_EPS)) * g


def _softplus(x):
    return jnp.maximum(x, 0.0) + jnp.log1p(jnp.exp(-jnp.abs(x)))


def _gelu_tanh(x):
    c = math.sqrt(2.0 / math.pi)
    return x * (0.5 * (1.0 + jnp.tanh(c * (x + 0.044715 * (x * x * x)))))


def _seg_sum(x):
    lane = lax.broadcasted_iota(jnp.int32, (1, LANES), 1)
    left = lane < HEAD
    outs = []
    for c in range(x.shape[1] // LANES):
        xc = x[:, LANES * c:LANES * (c + 1)]
        sl = jnp.sum(jnp.where(left, xc, 0.0), axis=-1, keepdims=True)
        sr = jnp.sum(jnp.where(left, 0.0, xc), axis=-1, keepdims=True)
        outs.append(jnp.where(left, sl, sr))
    return jnp.concatenate(outs, axis=1)


def _a_in_kernel(h_ref, g_ref, wg_ref, wx_ref, yg_ref, xr_ref):
    hn = _rms(h_ref[...], g_ref[...]).astype(BF16)
    yg_ref[...] = _gelu_tanh(_dot(hn, wg_ref[...])).astype(yg_ref.dtype)
    xr_ref[...] = _dot(hn, wx_ref[...])


def _a_scan_kernel(xr_ref, yg_ref, h_ref, cw_ref, cb_ref, gw_ref, gb_ref, lam_ref, wo_ref,
                   out_ref, cx_ref, ch_ref, *, tt):
    @pl.when(pl.program_id(1) == 0)
    def _():
        cx_ref[...] = jnp.zeros_like(cx_ref)
        ch_ref[...] = jnp.zeros_like(ch_ref)

    width = xr_ref.shape[1]
    blk = width // LRU_HEADS
    x = xr_ref[...]
    xe = jnp.concatenate([cx_ref[...], x], axis=0)
    cw = cw_ref[...]
    y = cb_ref[...] + cw[LRU_CONV - 1:LRU_CONV] * xe
    for k in range(1, LRU_CONV):
        y = y + cw[LRU_CONV - 1 - k:LRU_CONV - k] * pltpu.roll(xe, k, axis=0)
    y = y[HALO_F32:]
    cx_ref[...] = x[tt - HALO_F32:]

    yb = y.astype(BF16)
    rs, ins = [], []
    for hh in range(LRU_HEADS):
        gts = _dot(yb[:, blk * hh:blk * (hh + 1)], gw_ref[hh]) + gb_ref[:, 2 * blk * hh:2 * blk * (hh + 1)]
        rs.append(jax.nn.sigmoid(gts[:, :blk]))
        ins.append(jax.nn.sigmoid(gts[:, blk:]))
    r = jnp.concatenate(rs, axis=1)
    gi = jnp.concatenate(ins, axis=1)

    log_a = (-LRU_C * r) * _softplus(-lam_ref[...])
    a = jnp.exp(log_a)
    b = jnp.sqrt(-jnp.tanh(log_a) * (a * a + 1.0)) * (gi * y)

    row = lax.broadcasted_iota(jnp.int32, (tt, 1), 0)
    s = 1
    while s < tt:
        a_sh = pltpu.roll(a, s, axis=0)
        b_sh = pltpu.roll(b, s, axis=0)
        ok = row >= s
        b = jnp.where(ok, a * b_sh + b, b)
        a = jnp.where(ok, a * a_sh, a)
        s *= 2
    hs = b + a * ch_ref[0:1, :]
    ch_ref[...] = jnp.broadcast_to(hs[tt - 1:tt, :], ch_ref.shape)

    gated = (hs * yg_ref[...].astype(F32)).astype(BF16)
    out_ref[...] = h_ref[...] + _dot(gated, wo_ref[...])


def _rglru_layer(h, norm_g, w_in, conv_w, conv_b, gate_w, gate_b, lam, w_out, *, batch, seq):
    T, D = h.shape
    W = w_in.shape[1] // 2
    tm = 512
    w_in = w_in.astype(BF16)
    yg, xr = pl.pallas_call(
        _a_in_kernel,
        grid=(T // tm,),
        in_specs=[
            pl.BlockSpec((tm, D), lambda i: (i, 0)),
            _const_spec((1, D)),
            _const_spec((D, W)),
            _const_spec((D, W)),
        ],
        out_specs=[pl.BlockSpec((tm, W), lambda i: (i, 0)), pl.BlockSpec((tm, W), lambda i: (i, 0))],
        out_shape=[jax.ShapeDtypeStruct((T, W), BF16), jax.ShapeDtypeStruct((T, W), F32)],
        compiler_params=_params("parallel"),
        name="a_in",
    )(h, norm_g.reshape(1, D), w_in[:, :W], w_in[:, W:])

    tt = 256
    nt = seq // tt
    blk = W // LRU_HEADS
    row_spec = pl.BlockSpec((tt, W), lambda b, t: (b * nt + t, 0))
    return pl.pallas_call(
        functools.partial(_a_scan_kernel, tt=tt),
        grid=(batch, nt),
        in_specs=[
            row_spec, row_spec, pl.BlockSpec((tt, D), lambda b, t: (b * nt + t, 0)),
            _const_spec((LRU_CONV, W)),
            _const_spec((1, W)),
            _const_spec((LRU_HEADS, blk, 2 * blk)),
            _const_spec((1, 2 * W)),
            _const_spec((1, W)),
            _const_spec((W, D)),
        ],
        out_specs=pl.BlockSpec((tt, D), lambda b, t: (b * nt + t, 0)),
        out_shape=jax.ShapeDtypeStruct((T, D), F32),
        scratch_shapes=[pltpu.VMEM((HALO_F32, W), F32), pltpu.VMEM((HALO_F32, W), F32)],
        compiler_params=_params("parallel", "arbitrary"),
        name="a_scan",
    )(xr, yg, h, conv_w, conv_b.reshape(1, W), gate_w.astype(BF16), gate_b.reshape(1, 2 * W),
      lam.reshape(1, W), w_out.astype(BF16))


def _b_qkv_kernel(h_ref, g_ref, w_ref, cos_ref, sin_ref, *refs):
    outs, zs_ref = refs[:-1], refs[-1]
    G = len(DILATIONS)
    tm = h_ref.shape[0]
    hn = _rms(h_ref[...], g_ref[...]).astype(BF16)
    cos = cos_ref[...]
    sin = sin_ref[...]
    lane = lax.broadcasted_iota(jnp.int32, (1, LANES), 1)
    first_half = (lane % ATT_HEAD_DIM) < (ATT_HEAD_DIM // 2)
    blk = outs[0].shape[-1]

    def emit(z, dst_ref, d):
        if d == 1:
            dst_ref[0] = z.astype(dst_ref.dtype)
            return
        for c in range(blk // LANES):
            zs_ref[c] = z[:, LANES * c:LANES * (c + 1)]
        for r in range(d):
            for c in range(blk // LANES):
                part = zs_ref.at[c][pl.ds(r, tm // d, stride=d), :]
                dst_ref[r, :, LANES * c:LANES * (c + 1)] = part.astype(dst_ref.dtype)

    for j in range(2 * G + 1):
        z = _dot(hn, w_ref[:, blk * j:blk * (j + 1)])
        if j < 2 * G:
            cols = []
            for c in range(blk // LANES):
                zc = z[:, LANES * c:LANES * (c + 1)]
                up = pltpu.roll(zc, LANES - ATT_HEAD_DIM // 2, axis=1)
                dn = pltpu.roll(zc, ATT_HEAD_DIM // 2, axis=1)
                oc = zc * cos + jnp.where(first_half, up, dn) * sin
                if j < G:
                    oc = oc * (ATT_HEAD_DIM ** -0.5)
                cols.append(oc)
            z = jnp.concatenate(cols, axis=1)
            emit(z, outs[j], DILATIONS[j % G])
        else:
            for g in range(G):
                emit(z, outs[2 * G + g], DILATIONS[g])


def _b_attn_kernel(q_ref, kp_ref, kc_ref, vp_ref, vc_ref, o_ref, l_ref):
    n = pl.program_id(2)
    row = lax.broadcasted_iota(jnp.int32, (BAND, 2 * BAND), 0)
    col = lax.broadcasted_iota(jnp.int32, (BAND, 2 * BAND), 1)
    dist = BAND + row - col
    valid = (dist >= 0) & (dist <= BAND) & ((col >= BAND) | (n > 0))
    lane = lax.broadcasted_iota(jnp.int32, (1, LANES), 1)
    left = lane < ATT_HEAD_DIM
    zero = jnp.zeros((), BF16)
    for p in range(q_ref.shape[1] // LANES):
        sl = slice(LANES * p, LANES * (p + 1))
        q = q_ref[:, sl]
        k = jnp.concatenate([kp_ref[:, sl], kc_ref[:, sl]], axis=0)
        v = jnp.concatenate([vp_ref[:, sl], vc_ref[:, sl]], axis=0)
        outs, lses = [], []
        for hh in range(2):
            qm = jnp.where(left if hh == 0 else ~left, q, zero)
            s = jnp.where(valid, _dot_nt(qm, k), NEG_INF)
            m = jnp.max(s, axis=-1, keepdims=True)
            e = jnp.exp(s - m)
            den = jnp.sum(e, axis=-1, keepdims=True)
            outs.append(_dot(e.astype(BF16), v) / den)
            lses.append(m + jnp.log(den))
        o_ref[:, sl] = jnp.where(left, outs[0], outs[1]).astype(o_ref.dtype)
        l_ref[:, sl] = jnp.where(left, lses[0], lses[1])


def _b_out_kernel(o1_ref, o2_ref, o3_ref, l1_ref, l2_ref, l3_ref, h_ref, wo_ref, out_ref, *scratch):
    def rows(ref, scr):
        d, n = ref.shape[0], ref.shape[1]
        if d == 1:
            return ref[0].astype(F32)
        ncol = ref.shape[2] // LANES
        for r in range(d):
            part = ref[r].astype(F32)
            for c in range(ncol):
                scr.at[c][pl.ds(r, n, stride=d), :] = part[:, LANES * c:LANES * (c + 1)]
        return jnp.concatenate([scr[c] for c in range(ncol)], axis=1)

    l1, l2, l3 = rows(l1_ref, None), rows(l2_ref, scratch[0]), rows(l3_ref, scratch[1])
    m = jnp.maximum(jnp.maximum(l1, l2), l3)
    e1, e2, e3 = jnp.exp(l1 - m), jnp.exp(l2 - m), jnp.exp(l3 - m)
    o = e1 * rows(o1_ref, None) + e2 * rows(o2_ref, scratch[2]) + e3 * rows(o3_ref, scratch[3])
    o = o / (e1 + e2 + e3)
    out_ref[...] = h_ref[...] + _dot(o.astype(BF16), wo_ref[...])


def _attention_layer(h, norm_g, cos_t, sin_t, w_qkv, w_out, *, batch, seq):
    T, D = h.shape
    HD = w_out.shape[0]
    NQ = w_qkv.shape[1]
    G = len(DILATIONS)
    tm = 512
    tps = seq // tm

    def split_spec(d):
        return pl.BlockSpec((None, d, tm // d, HD), lambda i: (i // tps, 0, i % tps, 0))

    def split_shape(d, dtype):
        return jax.ShapeDtypeStruct((batch, d, seq // d, HD), dtype)

    qkv = pl.pallas_call(
        _b_qkv_kernel,
        grid=(T // tm,),
        in_specs=[
            pl.BlockSpec((tm, D), lambda i: (i, 0)),
            _const_spec((1, D)),
            _const_spec((D, NQ)),
            pl.BlockSpec((tm, LANES), lambda i: (i, 0)),
            pl.BlockSpec((tm, LANES), lambda i: (i, 0)),
        ],
        out_specs=[split_spec(d) for _ in range(3) for d in DILATIONS],
        out_shape=[split_shape(d, BF16) for _ in range(3) for d in DILATIONS],
        scratch_shapes=[pltpu.VMEM((HD // LANES, tm, LANES), F32)],
        compiler_params=_params("parallel"),
        name="b_qkv",
    )(h, norm_g.reshape(1, D), w_qkv.astype(BF16), cos_t, sin_t)

    outs, lses = [], []
    blk = (None, None, BAND, HD)
    cur = lambda b, r, n: (b, r, n, 0)
    prev = lambda b, r, n: (b, r, jnp.maximum(n - 1, 0), 0)
    for g, d in enumerate(DILATIONS):
        q_g, k_g, v_g = qkv[g], qkv[G + g], qkv[2 * G + g]
        o_g, l_g = pl.pallas_call(
            _b_attn_kernel,
            grid=(batch, d, seq // d // BAND),
            in_specs=[pl.BlockSpec(blk, cur), pl.BlockSpec(blk, prev), pl.BlockSpec(blk, cur),
                      pl.BlockSpec(blk, prev), pl.BlockSpec(blk, cur)],
            out_specs=[pl.BlockSpec(blk, cur), pl.BlockSpec(blk, cur)],
            out_shape=[split_shape(d, BF16), split_shape(d, F32)],
            compiler_params=_params("parallel", "parallel", "arbitrary"),
            name=f"b_attn_d{d}",
        )(q_g, k_g, k_g, v_g, v_g)
        outs.append(o_g)
        lses.append(l_g)

    row = lambda i: (i, 0)
    return pl.pallas_call(
        _b_out_kernel,
        grid=(T // tm,),
        in_specs=[split_spec(d) for _ in range(2) for d in DILATIONS]
        + [pl.BlockSpec((tm, D), row), _const_spec((HD, D))],
        out_specs=pl.BlockSpec((tm, D), row),
        out_shape=jax.ShapeDtypeStruct((T, D), F32),
        scratch_shapes=[pltpu.VMEM((HD // LANES, tm, LANES), F32)] * 4,
        compiler_params=_params("parallel"),
        name="b_out",
    )(*outs, *lses, h, w_out.astype(BF16))


def _c_in_kernel(h_ref, hh_ref, g_ref, mu_ref, wr_ref, wk_ref, wv_ref, w1_ref, w2_ref, a1_ref, a2_ref,
                 g1_ref, g2_ref, w0_ref, a0_ref, kk_ref, ka_ref, rk_ref,
                 at_out, bt_out, kt_out, rt_out, bh_out, kh_out, v_out, gam_out, g_out, bonus_out,
                 *, tiles_per_seq):
    g = g_ref[...]
    hn = _rms(h_ref[...], g)
    halo = _rms(hh_ref[...], g)
    halo = jnp.where(pl.program_id(0) % tiles_per_seq == 0, 0.0, halo)
    he = jnp.concatenate([halo, hn], axis=0)
    xx = pltpu.roll(he, 1, axis=0)[HALO_F32:] - hn
    mu = mu_ref[...]

    def mix(c):
        return (hn + xx * mu[c:c + 1]).astype(BF16)

    r = _dot(mix(0), wr_ref[...])
    k = _dot(mix(1), wk_ref[...])
    v = _dot(mix(2), wv_ref[...])
    wl = _dot(jnp.tanh(_dot(mix(3), w1_ref[...])).astype(BF16), w2_ref[...])
    al = _dot(_dot(mix(4), a1_ref[...]).astype(BF16), a2_ref[...])
    g_out[...] = _dot(jax.nn.sigmoid(_dot(mix(5), g1_ref[...])).astype(BF16), g2_ref[...])

    w = -_softplus(-(w0_ref[...] + wl)) - 0.5
    lw = -jnp.exp(w)
    a = jax.nn.sigmoid(a0_ref[...] + al)
    kk = k * kk_ref[...]
    kk = kk / jnp.maximum(jnp.sqrt(_seg_sum(kk * kk)), 1e-12)
    kh = k * (1.0 + (a - 1.0) * ka_ref[...])
    bb = kk * a
    bonus_out[...] = _seg_sum(r * kh * rk_ref[...]) * v
    v_out[...] = v.astype(v_out.dtype)

    tm = lw.shape[0]
    rowc = lax.broadcasted_iota(jnp.int32, (tm, 1), 0) % CHUNK
    cum = lw
    s = 1
    while s < CHUNK:
        cum = cum + jnp.where(rowc >= s, pltpu.roll(cum, s, axis=0), 0.0)
        s *= 2
    lasts = [jnp.broadcast_to(cum[CHUNK * (c + 1) - 1:CHUNK * (c + 1), :], (CHUNK, cum.shape[1]))
             for c in range(tm // CHUNK)]
    last = jnp.concatenate(lasts, axis=0)
    e_neg = jnp.exp(-cum)
    e_end = jnp.exp(last - cum)
    at_out[...] = (-kk * jnp.exp(cum - lw)).astype(at_out.dtype)
    bt_out[...] = (bb * e_neg).astype(bt_out.dtype)
    kt_out[...] = (kh * e_neg).astype(kt_out.dtype)
    rt_out[...] = (r * jnp.exp(cum)).astype(rt_out.dtype)
    bh_out[...] = (bb * e_end).astype(bh_out.dtype)
    kh_out[...] = (kh * e_end).astype(kh_out.dtype)
    for c in range(tm // CHUNK):
        gam_out[HALO_F32 * c:HALO_F32 * (c + 1), :] = jnp.exp(lasts[c][:HALO_F32])


def _pair_diag(x):
    lane = lax.broadcasted_iota(jnp.int32, (1, LANES), 1)
    left = lane < HEAD
    zero = jnp.zeros((), x.dtype)
    return jnp.concatenate([jnp.where(left, x, zero), jnp.where(left, zero, x)], axis=0).astype(BF16)


def _pair_tri_inv(ms):
    c = ms[0].shape[0]
    row = lax.broadcasted_iota(jnp.int32, ms[0].shape, 0)
    col = lax.broadcasted_iota(jnp.int32, ms[0].shape, 1) % HEAD
    eye = jnp.where(row == col, 1.0, 0.0)
    first = (row >> 1 == col >> 1) & (row > col)
    xs = [eye + jnp.where(first, m, 0.0) for m in ms]
    k = 1
    while (1 << k) < c:
        off = (row >> (k + 1) == col >> (k + 1)) & (((row >> k) & 1) == 1) & (((col >> k) & 1) == 0)
        zs = [_dot(x.astype(BF16), _pair_diag(jnp.where(off, m, 0.0))) for x, m in zip(xs, ms)]
        xs = [x + _dot(z.astype(BF16), _pair_diag(x)) for x, z in zip(xs, zs)]
        k += 1
    return xs


def _c_chunk_kernel(at_ref, bt_ref, kt_ref, rt_ref, kh_ref, v_ref, w_out, uv_out, yv_out, prb_out, vk_out):
    C = CHUNK
    row = lax.broadcasted_iota(jnp.int32, (C, LANES), 0)
    col = lax.broadcasted_iota(jnp.int32, (C, LANES), 1) % HEAD
    strict = row > col
    incl = row >= col
    lane = lax.broadcasted_iota(jnp.int32, (1, LANES), 1)
    left = lane < HEAD
    prow = lax.broadcasted_iota(jnp.int32, (LANES, LANES), 0)
    pcol = lax.broadcasted_iota(jnp.int32, (LANES, LANES), 1)
    same_head = (prow < HEAD) == (pcol < HEAD)
    zero = jnp.zeros((), BF16)
    pairs = [slice(LANES * p, LANES * (p + 1)) for p in range(at_ref.shape[1] // LANES)]

    at, bt, kt, rt, kh, v = at_ref[...], bt_ref[...], kt_ref[...], rt_ref[...], kh_ref[...], v_ref[...]
    bigs = []
    for sl in pairs:
        lhs = jnp.concatenate([at[:, sl], rt[:, sl]], axis=0)
        rhs = jnp.concatenate([jnp.where(left, bt[:, sl], zero), jnp.where(left, zero, bt[:, sl]),
                               jnp.where(left, kt[:, sl], zero), jnp.where(left, zero, kt[:, sl])], axis=0)
        bigs.append(_dot_nt(lhs, rhs))
    v_bds = [_pair_diag(v[:, sl]) for sl in pairs]
    gs = [_dot(jnp.where(strict, big[:C, LANES:], 0.0).astype(BF16), v_bd) for big, v_bd in zip(bigs, v_bds)]
    yv_out[...] = jnp.concatenate(
        [_dot(jnp.where(incl, big[C:, LANES:], 0.0).astype(BF16), v_bd) for big, v_bd in zip(bigs, v_bds)], axis=1)
    prb_out[...] = jnp.concatenate(
        [jnp.where(incl, big[C:, :LANES], 0.0) for big in bigs], axis=1).astype(prb_out.dtype)
    vk_out[...] = jnp.concatenate(
        [jnp.where(same_head, _dot(v[:, sl].astype(F32).T.astype(BF16), kh[:, sl]), 0.0) for sl in pairs], axis=0)
    t_invs = [t.astype(BF16) for t in _pair_tri_inv([jnp.where(strict, big[:C, :LANES], 0.0) for big in bigs])]
    w_out[...] = jnp.concatenate(
        [_dot(t, _pair_diag(at[:, sl])) for t, sl in zip(t_invs, pairs)], axis=1).astype(w_out.dtype)
    uv_out[...] = jnp.concatenate([_dot(t, _pair_diag(g)) for t, g in zip(t_invs, gs)], axis=1)


def _c_state_kernel(w_ref, uv_ref, rt_ref, yv_ref, bh_ref, prb_ref, vk_ref, gam_ref, y_out, s_ref):
    @pl.when(pl.program_id(1) == 0)
    def _():
        s_ref[...] = jnp.zeros_like(s_ref)

    C = w_ref.shape[0]
    prow = lax.broadcasted_iota(jnp.int32, (LANES, LANES), 0)
    pcol = lax.broadcasted_iota(jnp.int32, (LANES, LANES), 1)
    same_head = (prow < HEAD) == (pcol < HEAD)
    npair = w_ref.shape[1] // LANES
    pairs = [slice(LANES * p, LANES * (p + 1)) for p in range(npair)]
    w, uv, rt, yv, bh, prb = w_ref[...], uv_ref[...], rt_ref[...], yv_ref[...], bh_ref[...], prb_ref[...]
    gam = gam_ref[0:1, :]
    states = [s_ref[p] for p in range(npair)]
    urs = [_dot_nt(jnp.concatenate([w[:, sl], rt[:, sl]], axis=0), st.astype(BF16)) for sl, st in zip(pairs, states)]
    us = [ur[:C] + uv[:, sl] for ur, sl in zip(urs, pairs)]
    dss = [_dot(u.T.astype(BF16), bh[:, sl]) for u, sl in zip(us, pairs)]
    for p, sl in enumerate(pairs):
        s_ref[p] = states[p] * gam[:, sl] + jnp.where(same_head, dss[p], 0.0) + vk_ref[LANES * p:LANES * (p + 1), :]
    pus = [_dot(prb[:, sl], _pair_diag(u)) for u, sl in zip(us, pairs)]
    y_out[...] = jnp.concatenate([ur[C:] + pu + yv[:, sl] for ur, pu, sl in zip(urs, pus, pairs)], axis=1)


def _c_out_kernel(y_ref, bonus_ref, g_ref, h_ref, lnw_ref, lnb_ref, wo_ref, out_ref):
    y = y_ref[...]
    mean = _seg_sum(y) * (1.0 / HEAD)
    yc = y - mean
    var = _seg_sum(yc * yc) * (1.0 / HEAD)
    yn = (yc * lax.rsqrt(var + GN_EPS)) * lnw_ref[...] + lnb_ref[...]
    out = ((yn + bonus_ref[...]) * g_ref[...]).astype(BF16)
    out_ref[...] = h_ref[...] + _dot(out, wo_ref[...])


def _rwkv_layer(h, norm_g, mu, w_rkv, w0, w1, w2, a0, a1, a2, g1, g2, k_k, k_a, r_k, ln_w, ln_b, w_out,
                *, batch, seq):
    T, D = h.shape
    C = CHUNK
    tm = 256
    nc = T // C
    npair = D // LANES
    row = lambda i: (i, 0)
    vec = lambda x: x.reshape(1, D)
    halo_spec = pl.BlockSpec((HALO_F32, D), lambda i: (jnp.maximum(i * (tm // HALO_F32) - 1, 0), 0))
    w_rkv = w_rkv.astype(BF16)
    consts = [vec(norm_g), mu, w_rkv[0], w_rkv[1], w_rkv[2], w1.astype(BF16), w2.astype(BF16), a1.astype(BF16),
              a2.astype(BF16), g1.astype(BF16), g2.astype(BF16), vec(w0), vec(a0), vec(k_k), vec(k_a), vec(r_k)]
    tile = pl.BlockSpec((tm, D), row)
    gam_spec = pl.BlockSpec((tm // C * HALO_F32, D), row)
    gam_shape = jax.ShapeDtypeStruct((nc * HALO_F32, D), F32)
    at, bt, kt, rt, bh, khat, v, gam, gg, bonus = pl.pallas_call(
        functools.partial(_c_in_kernel, tiles_per_seq=seq // tm),
        grid=(T // tm,),
        in_specs=[tile, halo_spec] + [_const_spec(c.shape) for c in consts],
        out_specs=[tile] * 7 + [gam_spec, tile, tile],
        out_shape=[jax.ShapeDtypeStruct((T, D), BF16)] * 7 + [gam_shape] + [jax.ShapeDtypeStruct((T, D), F32)] * 2,
        compiler_params=_params("parallel"),
        name="c_in",
    )(h, h, *consts)

    ctile = pl.BlockSpec((C, D), row)
    wq, uv, yv, prb, vk = pl.pallas_call(
        _c_chunk_kernel,
        grid=(nc,),
        in_specs=[ctile] * 6,
        out_specs=[ctile] * 4 + [pl.BlockSpec((npair * LANES, LANES), row)],
        out_shape=[jax.ShapeDtypeStruct((T, D), BF16), jax.ShapeDtypeStruct((T, D), F32),
                   jax.ShapeDtypeStruct((T, D), F32), jax.ShapeDtypeStruct((T, D), BF16),
                   jax.ShapeDtypeStruct((nc * npair * LANES, LANES), F32)],
        compiler_params=_params("parallel"),
        name="c_chunk",
    )(at, bt, kt, rt, khat, v)

    cps = seq // C
    srow = lambda b, c: (b * cps + c, 0)
    crow = pl.BlockSpec((C, D), srow)
    y = pl.pallas_call(
        _c_state_kernel,
        grid=(batch, cps),
        in_specs=[crow] * 6 + [pl.BlockSpec((npair * LANES, LANES), srow), pl.BlockSpec((HALO_F32, D), srow)],
        out_specs=crow,
        out_shape=jax.ShapeDtypeStruct((T, D), F32),
        scratch_shapes=[pltpu.VMEM((npair, LANES, LANES), F32)],
        compiler_params=_params("parallel", "arbitrary"),
        name="c_state",
    )(wq, uv, rt, yv, bh, prb, vk, gam)

    tmo = 512
    otile = pl.BlockSpec((tmo, D), row)
    return pl.pallas_call(
        _c_out_kernel,
        grid=(T // tmo,),
        in_specs=[otile] * 4 + [_const_spec((1, D))] * 2 + [_const_spec((D, D))],
        out_specs=otile,
        out_shape=jax.ShapeDtypeStruct((T, D), F32),
        compiler_params=_params("parallel"),
        name="c_out",
    )(y, bonus, gg, h, vec(ln_w), vec(ln_b), w_out.astype(BF16))


def _ffn_kernel(h_ref, hh_ref, g_ref, wg_ref, wu_ref, cwg_ref, cwu_ref, cbg_ref, cbu_ref, wd_ref,
                out_ref, hn_ref, act_ref, *, tiles_per_seq, fc):
    g = g_ref[...]
    halo = _rms(hh_ref[...], g)
    halo = jnp.where(pl.program_id(0) % tiles_per_seq == 0, 0.0, halo)
    hn_ref[:HALO_BF16, :] = halo.astype(BF16)
    hn_ref[HALO_BF16:, :] = _rms(h_ref[...], g).astype(BF16)

    def conv(u, cw, cb):
        y = cb + cw[2:3] * u + cw[1:2] * pltpu.roll(u, 1, axis=0) + cw[0:1] * pltpu.roll(u, 2, axis=0)
        return y[HALO_BF16:]

    for c in range(act_ref.shape[1] // fc):
        sl = slice(fc * c, fc * (c + 1))
        hn = hn_ref[...]
        yg = conv(_dot(hn, wg_ref[:, sl]), cwg_ref[:, sl], cbg_ref[:, sl])
        yu = conv(_dot(hn, wu_ref[:, sl]), cwu_ref[:, sl], cbu_ref[:, sl])
        act_ref[:, sl] = ((yg * jax.nn.sigmoid(yg)) * yu).astype(BF16)
    out_ref[...] = h_ref[...] + _dot(act_ref[...], wd_ref[...])


def _ffn_layer(h, norm_g, w_up, conv_w, conv_b, w_down, *, seq):
    T, D = h.shape
    F = w_down.shape[0]
    fc = 256
    tm = 512
    w_up = w_up.astype(BF16)
    consts = [norm_g.reshape(1, D), w_up[:, :F], w_up[:, F:], conv_w[:, :F], conv_w[:, F:],
              conv_b[None, :F], conv_b[None, F:], w_down.astype(BF16)]
    halo_spec = pl.BlockSpec((HALO_BF16, D), lambda i: (jnp.maximum(i * (tm // HALO_BF16) - 1, 0), 0))
    return pl.pallas_call(
        functools.partial(_ffn_kernel, tiles_per_seq=seq // tm, fc=fc),
        grid=(T // tm,),
        in_specs=[pl.BlockSpec((tm, D), lambda i: (i, 0)), halo_spec] + [_const_spec(c.shape) for c in consts],
        out_specs=pl.BlockSpec((tm, D), lambda i: (i, 0)),
        out_shape=jax.ShapeDtypeStruct((T, D), F32),
        scratch_shapes=[pltpu.VMEM((tm + HALO_BF16, D), BF16), pltpu.VMEM((tm, F), BF16)],
        compiler_params=_params("parallel"),
        name="ffn",
    )(h, h, *consts)


def _ple_kernel(h_ref, p_ref, g_ref, wg_ref, wp_ref, gf_ref, out_ref, *, final_norm):
    h = h_ref[...]
    hn = _rms(h, g_ref[...]).astype(BF16)
    gate = jax.nn.sigmoid(_dot(hn, wg_ref[...]))
    out = h + gate * _dot(p_ref[...].astype(BF16), wp_ref[...])
    if final_norm:
        out = _rms(out, gf_ref[...])
    out_ref[...] = out


def _ple_layer(h, p, layer, norm_g, w_gate, w_proj, norm_final, *, final_norm):
    T, D = h.shape
    P = p.shape[-1]
    tm = 512
    return pl.pallas_call(
        functools.partial(_ple_kernel, final_norm=final_norm),
        grid=(T // tm,),
        in_specs=[
            pl.BlockSpec((tm, D), lambda i: (i, 0)),
            pl.BlockSpec((None, tm, P), lambda i: (layer, i, 0)),
            _const_spec((1, D)),
            _const_spec((D, D)),
            _const_spec((P, D)),
            _const_spec((1, D)),
        ],
        out_specs=pl.BlockSpec((tm, D), lambda i: (i, 0)),
        out_shape=jax.ShapeDtypeStruct((T, D), F32),
        compiler_params=_params("parallel"),
        name="ple",
    )(h, p, norm_g.reshape(1, D), w_gate.astype(BF16), w_proj.astype(BF16), norm_final.reshape(1, D))


def _rotary_tables(positions):
    inv_freq = ROPE_THETA ** (-jnp.arange(0, ATT_HEAD_DIM, 2, dtype=F32) / ATT_HEAD_DIM)
    ang = positions.astype(F32).reshape(-1, 1) * inv_freq
    cos, sin = jnp.cos(ang), jnp.sin(ang)
    reps = LANES // ATT_HEAD_DIM
    cos_t = jnp.tile(jnp.concatenate([cos, cos], axis=1), (1, reps))
    sin_t = jnp.tile(jnp.concatenate([-sin, sin], axis=1), (1, reps))
    return cos_t, sin_t


def kernel(x, p, positions, norm_mix, norm_ffn, norm_ple, norm_final, a_w_in, a_conv_w, a_conv_b, a_gate_w, a_gate_b, a_lambda, a_w_out, b_w_qkv, b_w_out, c_mu, c_w_rkv, c_w0, c_w1, c_w2, c_a0, c_a1, c_a2, c_g1, c_g2, c_k_k, c_k_a, c_r_k, c_ln_w, c_ln_b, c_w_out, f_w_up, f_conv_w, f_conv_b, f_w_down, ple_w_proj, ple_w_gate):
    batch, seq, D = x.shape
    depth = norm_mix.shape[0]
    T = batch * seq
    h = x.reshape(T, D)
    pf = p.reshape(depth, T, p.shape[-1])
    cos_t, sin_t = _rotary_tables(positions)
    for i in range(depth):
        kind, j = i % 3, i // 3
        if kind == 0:
            h = _rglru_layer(h, norm_mix[i], a_w_in[j], a_conv_w[j], a_conv_b[j], a_gate_w[j], a_gate_b[j],
                             a_lambda[j], a_w_out[j], batch=batch, seq=seq)
        elif kind == 1:
            h = _attention_layer(h, norm_mix[i], cos_t, sin_t, b_w_qkv[j], b_w_out[j], batch=batch, seq=seq)
        else:
            h = _rwkv_layer(h, norm_mix[i], c_mu[j], c_w_rkv[j], c_w0[j], c_w1[j], c_w2[j], c_a0[j], c_a1[j],
                            c_a2[j], c_g1[j], c_g2[j], c_k_k[j], c_k_a[j], c_r_k[j], c_ln_w[j], c_ln_b[j],
                            c_w_out[j], batch=batch, seq=seq)
        h = _ffn_layer(h, norm_ffn[i], f_w_up[i], f_conv_w[i], f_conv_b[i], f_w_down[i], seq=seq)
        h = _ple_layer(h, pf, i, norm_ple[i], ple_w_gate[i], ple_w_proj[i], norm_final,
                       final_norm=(i == depth - 1))
    return h.reshape(batch, seq, D)
```

```python
import functools
import math

import jax
import jax.numpy as jnp
from jax import lax
from jax.experimental import pallas as pl
from jax.experimental.pallas import tpu as pltpu

F32 = jnp.float32
BF16 = jnp.bfloat16

RMS_EPS = 1e-6
LRU_HEADS = 4
LRU_CONV = 4
LRU_C = 8.0
ATT_HEAD_DIM = 64
DILATIONS = (1, 4, 16)
BAND = 128
ROPE_THETA = 10000.0
NEG_INF = -1e30
HEAD = 64
GN_EPS = 64e-5
CHUNK = 64

LANES = 128
HALO_BF16 = 16
HALO_F32 = 8
VMEM_LIMIT = 56 * 1024 * 1024


def _params(*sem):
    return pltpu.CompilerParams(dimension_semantics=sem, vmem_limit_bytes=VMEM_LIMIT)


def _const_spec(shape):
    nd = len(shape)
    return pl.BlockSpec(shape, lambda *_: (0,) * nd, pipeline_mode=pl.Buffered(1))


def _dot(a, b):
    return jnp.dot(a, b, preferred_element_type=F32)


def _dot_nt(a, b):
    return lax.dot_general(a, b, (((1,), (1,)), ((), ())), preferred_element_type=F32)


def _rms(x, g):
    ms = jnp.mean(x * x, axis=-1, keepdims=True)
    return (x * lax.rsqrt(ms + RMS_EPS)) * g


def _softplus(x):
    return jnp.maximum(x, 0.0) + jnp.log1p(jnp.exp(-jnp.abs(x)))


def _gelu_tanh(x):
    c = math.sqrt(2.0 / math.pi)
    return x * (0.5 * (1.0 + jnp.tanh(c * (x + 0.044715 * (x * x * x)))))


def _seg_sum(x):
    lane = lax.broadcasted_iota(jnp.int32, (1, LANES), 1)
    left = lane < HEAD
    outs = []
    for c in range(x.shape[1] // LANES):
        xc = x[:, LANES * c:LANES * (c + 1)]
        sl = jnp.sum(jnp.where(left, xc, 0.0), axis=-1, keepdims=True)
        sr = jnp.sum(jnp.where(left, 0.0, xc), axis=-1, keepdims=True)
        outs.append(jnp.where(left, sl, sr))
    return jnp.concatenate(outs, axis=1)


def _a_in_kernel(h_ref, g_ref, wg_ref, wx_ref, yg_ref, xr_ref):
    hn = _rms(h_ref[...], g_ref[...]).astype(BF16)
    yg_ref[...] = _gelu_tanh(_dot(hn, wg_ref[...])).astype(yg_ref.dtype)
    xr_ref[...] = _dot(hn, wx_ref[...])


def _a_scan_kernel(xr_ref, yg_ref, h_ref, cw_ref, cb_ref, gw_ref, gb_ref, lam_ref, wo_ref,
                   out_ref, cx_ref, ch_ref, *, tt):
    @pl.when(pl.program_id(1) == 0)
    def _():
        cx_ref[...] = jnp.zeros_like(cx_ref)
        ch_ref[...] = jnp.zeros_like(ch_ref)

    width = xr_ref.shape[1]
    blk = width // LRU_HEADS
    x = xr_ref[...]
    xe = jnp.concatenate([cx_ref[...], x], axis=0)
    cw = cw_ref[...]
    y = cb_ref[...] + cw[LRU_CONV - 1:LRU_CONV] * xe
    for k in range(1, LRU_CONV):
        y = y + cw[LRU_CONV - 1 - k:LRU_CONV - k] * pltpu.roll(xe, k, axis=0)
    y = y[HALO_F32:]
    cx_ref[...] = x[tt - HALO_F32:]

    yb = y.astype(BF16)
    rs, ins = [], []
    for hh in range(LRU_HEADS):
        gts = _dot(yb[:, blk * hh:blk * (hh + 1)], gw_ref[hh]) + gb_ref[:, 2 * blk * hh:2 * blk * (hh + 1)]
        rs.append(jax.nn.sigmoid(gts[:, :blk]))
        ins.append(jax.nn.sigmoid(gts[:, blk:]))
    r = jnp.concatenate(rs, axis=1)
    gi = jnp.concatenate(ins, axis=1)

    log_a = (-LRU_C * r) * _softplus(-lam_ref[...])
    a = jnp.exp(log_a)
    b = jnp.sqrt(-jnp.tanh(log_a) * (a * a + 1.0)) * (gi * y)

    row = lax.broadcasted_iota(jnp.int32, (HALO_F32, 1), 0)
    carry = ch_ref[0:1, :]
    groups = []
    for g0 in range(0, tt, HALO_F32):
        ag, bg = a[g0:g0 + HALO_F32], b[g0:g0 + HALO_F32]
        s = 1
        while s < HALO_F32:
            ok = row >= s
            bg = jnp.where(ok, ag * pltpu.roll(bg, s, axis=0) + bg, bg)
            ag = jnp.where(ok, ag * pltpu.roll(ag, s, axis=0), ag)
            s *= 2
        hg = bg + ag * carry
        carry = hg[HALO_F32 - 1:HALO_F32, :]
        groups.append(hg)
    hs = jnp.concatenate(groups, axis=0)
    ch_ref[...] = jnp.broadcast_to(carry, ch_ref.shape)

    gated = (hs * yg_ref[...].astype(F32)).astype(BF16)
    out_ref[...] = h_ref[...] + _dot(gated, wo_ref[...])


def _rglru_layer(h, norm_g, w_in, conv_w, conv_b, gate_w, gate_b, lam, w_out, *, batch, seq):
    T, D = h.shape
    W = w_in.shape[1] // 2
    tm = 512
    w_in = w_in.astype(BF16)
    yg, xr = pl.pallas_call(
        _a_in_kernel,
        grid=(T // tm,),
        in_specs=[
            pl.BlockSpec((tm, D), lambda i: (i, 0)),
            _const_spec((1, D)),
            _const_spec((D, W)),
            _const_spec((D, W)),
        ],
        out_specs=[pl.BlockSpec((tm, W), lambda i: (i, 0)), pl.BlockSpec((tm, W), lambda i: (i, 0))],
        out_shape=[jax.ShapeDtypeStruct((T, W), BF16), jax.ShapeDtypeStruct((T, W), F32)],
        compiler_params=_params("parallel"),
        name="a_in",
    )(h, norm_g.reshape(1, D), w_in[:, :W], w_in[:, W:])

    tt = 256
    nt = seq // tt
    blk = W // LRU_HEADS
    row_spec = pl.BlockSpec((tt, W), lambda b, t: (b * nt + t, 0))
    return pl.pallas_call(
        functools.partial(_a_scan_kernel, tt=tt),
        grid=(batch, nt),
        in_specs=[
            row_spec, row_spec, pl.BlockSpec((tt, D), lambda b, t: (b * nt + t, 0)),
            _const_spec((LRU_CONV, W)),
            _const_spec((1, W)),
            _const_spec((LRU_HEADS, blk, 2 * blk)),
            _const_spec((1, 2 * W)),
            _const_spec((1, W)),
            _const_spec((W, D)),
        ],
        out_specs=pl.BlockSpec((tt, D), lambda b, t: (b * nt + t, 0)),
        out_shape=jax.ShapeDtypeStruct((T, D), F32),
        scratch_shapes=[pltpu.VMEM((HALO_F32, W), F32), pltpu.VMEM((HALO_F32, W), F32)],
        compiler_params=_params("parallel", "arbitrary"),
        name="a_scan",
    )(xr, yg, h, conv_w, conv_b.reshape(1, W), gate_w.astype(BF16), gate_b.reshape(1, 2 * W),
      lam.reshape(1, W), w_out.astype(BF16))


def _b_qkv_kernel(h_ref, g_ref, w_ref, cos_ref, sin_ref, *refs):
    outs, zs_ref = refs[:-1], refs[-1]
    G = len(DILATIONS)
    tm = h_ref.shape[0]
    hn = _rms(h_ref[...], g_ref[...]).astype(BF16)
    cos = cos_ref[...]
    sin = sin_ref[...]
    lane = lax.broadcasted_iota(jnp.int32, (1, LANES), 1)
    first_half = (lane % ATT_HEAD_DIM) < (ATT_HEAD_DIM // 2)
    blk = outs[0].shape[-1]

    def emit(z, dst_ref, d):
        if d == 1:
            dst_ref[0] = z.astype(dst_ref.dtype)
            return
        for c in range(blk // LANES):
            zs_ref[c] = z[:, LANES * c:LANES * (c + 1)]
        for r in range(d):
            for c in range(blk // LANES):
                part = zs_ref.at[c][pl.ds(r, tm // d, stride=d), :]
                dst_ref[r, :, LANES * c:LANES * (c + 1)] = part.astype(dst_ref.dtype)

    for j in range(2 * G + 1):
        z = _dot(hn, w_ref[:, blk * j:blk * (j + 1)])
        if j < 2 * G:
            cols = []
            for c in range(blk // LANES):
                zc = z[:, LANES * c:LANES * (c + 1)]
                up = pltpu.roll(zc, LANES - ATT_HEAD_DIM // 2, axis=1)
                dn = pltpu.roll(zc, ATT_HEAD_DIM // 2, axis=1)
                oc = zc * cos + jnp.where(first_half, up, dn) * sin
                if j < G:
                    oc = oc * (ATT_HEAD_DIM ** -0.5)
                cols.append(oc)
            z = jnp.concatenate(cols, axis=1)
            emit(z, outs[j], DILATIONS[j % G])
        else:
            for g in range(G):
                emit(z, outs[2 * G + g], DILATIONS[g])


def _b_attn_kernel(q_ref, kp_ref, kc_ref, vp_ref, vc_ref, o_ref, l_ref):
    n = pl.program_id(2)
    row = lax.broadcasted_iota(jnp.int32, (BAND, 2 * BAND), 0)
    col = lax.broadcasted_iota(jnp.int32, (BAND, 2 * BAND), 1)
    dist = BAND + row - col
    valid = (dist >= 0) & (dist <= BAND) & ((col >= BAND) | (n > 0))
    lane = lax.broadcasted_iota(jnp.int32, (1, LANES), 1)
    left = lane < ATT_HEAD_DIM
    zero = jnp.zeros((), BF16)
    pairs = [slice(LANES * p, LANES * (p + 1)) for p in range(q_ref.shape[1] // LANES)]
    q, kp, kc, vp, vc = q_ref[...], kp_ref[...], kc_ref[...], vp_ref[...], vc_ref[...]
    ks = [jnp.concatenate([kp[:, sl], kc[:, sl]], axis=0) for sl in pairs]
    vs = [jnp.concatenate([vp[:, sl], vc[:, sl]], axis=0) for sl in pairs]
    heads = [(p, hh) for p in range(len(pairs)) for hh in range(2)]
    ss = [jnp.where(valid, _dot_nt(jnp.where(left if hh == 0 else ~left, q[:, pairs[p]], zero), ks[p]), NEG_INF)
          for p, hh in heads]
    ms = [jnp.max(s, axis=-1, keepdims=True) for s in ss]
    es = [jnp.exp(s - m) for s, m in zip(ss, ms)]
    dens = [jnp.sum(e, axis=-1, keepdims=True) for e in es]
    pvs = [_dot(e.astype(BF16), vs[p]) for e, (p, _) in zip(es, heads)]
    outs = [pv / den for pv, den in zip(pvs, dens)]
    lses = [m + jnp.log(den) for m, den in zip(ms, dens)]
    o_ref[...] = jnp.concatenate(
        [jnp.where(left, outs[2 * p], outs[2 * p + 1]) for p in range(len(pairs))], axis=1).astype(o_ref.dtype)
    l_ref[...] = jnp.concatenate(
        [jnp.where(left, lses[2 * p], lses[2 * p + 1]) for p in range(len(pairs))], axis=1)


def _b_out_kernel(o1_ref, o2_ref, o3_ref, l1_ref, l2_ref, l3_ref, h_ref, wo_ref, out_ref, *scratch):
    def rows(ref, scr):
        d, n = ref.shape[0], ref.shape[1]
        if d == 1:
            return ref[0].astype(F32)
        ncol = ref.shape[2] // LANES
        for r in range(d):
            part = ref[r].astype(F32)
            for c in range(ncol):
                scr.at[c][pl.ds(r, n, stride=d), :] = part[:, LANES * c:LANES * (c + 1)]
        return jnp.concatenate([scr[c] for c in range(ncol)], axis=1)

    l1, l2, l3 = rows(l1_ref, None), rows(l2_ref, scratch[0]), rows(l3_ref, scratch[1])
    m = jnp.maximum(jnp.maximum(l1, l2), l3)
    e1, e2, e3 = jnp.exp(l1 - m), jnp.exp(l2 - m), jnp.exp(l3 - m)
    o = e1 * rows(o1_ref, None) + e2 * rows(o2_ref, scratch[2]) + e3 * rows(o3_ref, scratch[3])
    o = o / (e1 + e2 + e3)
    out_ref[...] = h_ref[...] + _dot(o.astype(BF16), wo_ref[...])


def _attention_layer(h, norm_g, cos_t, sin_t, w_qkv, w_out, *, batch, seq):
    T, D = h.shape
    HD = w_out.shape[0]
    NQ = w_qkv.shape[1]
    G = len(DILATIONS)
    tm = 512
    tps = seq // tm

    def split_spec(d):
        return pl.BlockSpec((None, d, tm // d, HD), lambda i: (i // tps, 0, i % tps, 0))

    def split_shape(d, dtype):
        return jax.ShapeDtypeStruct((batch, d, seq // d, HD), dtype)

    qkv = pl.pallas_call(
        _b_qkv_kernel,
        grid=(T // tm,),
        in_specs=[
            pl.BlockSpec((tm, D), lambda i: (i, 0)),
            _const_spec((1, D)),
            _const_spec((D, NQ)),
            pl.BlockSpec((tm, LANES), lambda i: (i, 0)),
            pl.BlockSpec((tm, LANES), lambda i: (i, 0)),
        ],
        out_specs=[split_spec(d) for _ in range(3) for d in DILATIONS],
        out_shape=[split_shape(d, BF16) for _ in range(3) for d in DILATIONS],
        scratch_shapes=[pltpu.VMEM((HD // LANES, tm, LANES), F32)],
        compiler_params=_params("parallel"),
        name="b_qkv",
    )(h, norm_g.reshape(1, D), w_qkv.astype(BF16), cos_t, sin_t)

    outs, lses = [], []
    blk = (None, None, BAND, HD)
    cur = lambda b, r, n: (b, r, n, 0)
    prev = lambda b, r, n: (b, r, jnp.maximum(n - 1, 0), 0)
    for g, d in enumerate(DILATIONS):
        q_g, k_g, v_g = qkv[g], qkv[G + g], qkv[2 * G + g]
        o_g, l_g = pl.pallas_call(
            _b_attn_kernel,
            grid=(batch, d, seq // d // BAND),
            in_specs=[pl.BlockSpec(blk, cur), pl.BlockSpec(blk, prev), pl.BlockSpec(blk, cur),
                      pl.BlockSpec(blk, prev), pl.BlockSpec(blk, cur)],
            out_specs=[pl.BlockSpec(blk, cur), pl.BlockSpec(blk, cur)],
            out_shape=[split_shape(d, BF16), split_shape(d, F32)],
            compiler_params=_params("parallel", "parallel", "arbitrary"),
            name=f"b_attn_d{d}",
        )(q_g, k_g, k_g, v_g, v_g)
        outs.append(o_g)
        lses.append(l_g)

    row = lambda i: (i, 0)
    return pl.pallas_call(
        _b_out_kernel,
        grid=(T // tm,),
        in_specs=[split_spec(d) for _ in range(2) for d in DILATIONS]
        + [pl.BlockSpec((tm, D), row), _const_spec((HD, D))],
        out_specs=pl.BlockSpec((tm, D), row),
        out_shape=jax.ShapeDtypeStruct((T, D), F32),
        scratch_shapes=[pltpu.VMEM((HD // LANES, tm, LANES), F32)] * 4,
        compiler_params=_params("parallel"),
        name="b_out",
    )(*outs, *lses, h, w_out.astype(BF16))


def _c_in_kernel(h_ref, hh_ref, g_ref, mu_ref, wr_ref, wk_ref, wv_ref, w1_ref, w2_ref, a1_ref, a2_ref,
                 g1_ref, g2_ref, w0_ref, a0_ref, kk_ref, ka_ref, rk_ref,
                 rt_out, bh_out, gam_out, g_out, bonus_out, w_out, uv_out, yv_out, prb_out, vk_out,
                 *, tiles_per_seq):
    g = g_ref[...]
    hn = _rms(h_ref[...], g)
    halo = _rms(hh_ref[...], g)
    halo = jnp.where(pl.program_id(0) % tiles_per_seq == 0, 0.0, halo)
    he = jnp.concatenate([halo, hn], axis=0)
    xx = pltpu.roll(he, 1, axis=0)[HALO_F32:] - hn
    mu = mu_ref[...]

    def mix(c):
        return (hn + xx * mu[c:c + 1]).astype(BF16)

    r = _dot(mix(0), wr_ref[...])
    k = _dot(mix(1), wk_ref[...])
    v = _dot(mix(2), wv_ref[...])
    wl = _dot(jnp.tanh(_dot(mix(3), w1_ref[...])).astype(BF16), w2_ref[...])
    al = _dot(_dot(mix(4), a1_ref[...]).astype(BF16), a2_ref[...])
    g_out[...] = _dot(jax.nn.sigmoid(_dot(mix(5), g1_ref[...])).astype(BF16), g2_ref[...])

    z = -(w0_ref[...] + wl)
    w = -(jnp.maximum(z, 0.0) + jnp.log(1.0 + jnp.exp(-jnp.abs(z)))) - 0.5
    lw = -jnp.exp(w)
    a = jax.nn.sigmoid(a0_ref[...] + al)
    kk = k * kk_ref[...]
    kk = kk / jnp.maximum(jnp.sqrt(_seg_sum(kk * kk)), 1e-12)
    kh = k * (1.0 + (a - 1.0) * ka_ref[...])
    bb = kk * a
    bonus_out[...] = _seg_sum(r * kh * rk_ref[...]) * v

    tm = lw.shape[0]
    rowc = lax.broadcasted_iota(jnp.int32, (tm, 1), 0) % CHUNK
    cum = lw
    s = 1
    while s < CHUNK:
        cum = cum + jnp.where(rowc >= s, pltpu.roll(cum, s, axis=0), 0.0)
        s *= 2
    lasts = [jnp.broadcast_to(cum[CHUNK * (c + 1) - 1:CHUNK * (c + 1), :], (CHUNK, cum.shape[1]))
             for c in range(tm // CHUNK)]
    last = jnp.concatenate(lasts, axis=0)
    e_neg = jnp.exp(-cum)
    e_end = jnp.exp(last - cum)
    at = (-kk * jnp.exp(cum - lw)).astype(BF16)
    bt = (bb * e_neg).astype(BF16)
    kt = (kh * e_neg).astype(BF16)
    rt = (r * jnp.exp(cum)).astype(BF16)
    rt_out[...] = rt
    bh_out[...] = (bb * e_end).astype(bh_out.dtype)
    for c in range(tm // CHUNK):
        gam_out[HALO_F32 * c:HALO_F32 * (c + 1), :] = jnp.exp(lasts[c][:HALO_F32])
    _chunk_algebra(at, bt, kt, rt, (kh * e_end).astype(BF16), v.astype(BF16), w_out, uv_out, yv_out, prb_out, vk_out)


def _pair_diag(x):
    lane = lax.broadcasted_iota(jnp.int32, (1, LANES), 1)
    left = lane < HEAD
    zero = jnp.zeros((), x.dtype)
    return jnp.concatenate([jnp.where(left, x, zero), jnp.where(left, zero, x)], axis=0).astype(BF16)


def _pair_tri_inv(ms):
    c = ms[0].shape[0]
    row = lax.broadcasted_iota(jnp.int32, ms[0].shape, 0)
    col = lax.broadcasted_iota(jnp.int32, ms[0].shape, 1) % HEAD
    eye = jnp.where(row == col, 1.0, 0.0)
    first = (row >> 1 == col >> 1) & (row > col)
    xs = [eye + jnp.where(first, m, 0.0) for m in ms]
    k = 1
    while (1 << k) < c:
        off = (row >> (k + 1) == col >> (k + 1)) & (((row >> k) & 1) == 1) & (((col >> k) & 1) == 0)
        zs = [_dot(x.astype(BF16), _pair_diag(jnp.where(off, m, 0.0))) for x, m in zip(xs, ms)]
        xs = [x + _dot(z.astype(BF16), _pair_diag(x)) for x, z in zip(xs, zs)]
        k += 1
    return xs


def _chunk_algebra(at, bt, kt, rt, kh, v, w_out, uv_out, yv_out, prb_out, vk_out):
    C = CHUNK
    row = lax.broadcasted_iota(jnp.int32, (C, LANES), 0)
    col = lax.broadcasted_iota(jnp.int32, (C, LANES), 1) % HEAD
    strict = row > col
    incl = row >= col
    lane = lax.broadcasted_iota(jnp.int32, (1, LANES), 1)
    left = lane < HEAD
    prow = lax.broadcasted_iota(jnp.int32, (LANES, LANES), 0)
    pcol = lax.broadcasted_iota(jnp.int32, (LANES, LANES), 1)
    same_head = (prow < HEAD) == (pcol < HEAD)
    zero = jnp.zeros((), BF16)
    npair = at.shape[1] // LANES
    nchunk = at.shape[0] // C
    items = [(slice(C * c, C * (c + 1)), slice(LANES * p, LANES * (p + 1))) for c in range(nchunk) for p in range(npair)]
    bigs = []
    for rs, sl in items:
        lhs = jnp.concatenate([at[rs, sl], rt[rs, sl]], axis=0)
        rhs = jnp.concatenate([jnp.where(left, bt[rs, sl], zero), jnp.where(left, zero, bt[rs, sl]),
                               jnp.where(left, kt[rs, sl], zero), jnp.where(left, zero, kt[rs, sl])], axis=0)
        bigs.append(_dot_nt(lhs, rhs))
    v_bds = [_pair_diag(v[rs, sl]) for rs, sl in items]
    gs = [_dot(jnp.where(strict, big[:C, LANES:], 0.0).astype(BF16), v_bd) for big, v_bd in zip(bigs, v_bds)]
    yvs = [_dot(jnp.where(incl, big[C:, LANES:], 0.0).astype(BF16), v_bd) for big, v_bd in zip(bigs, v_bds)]
    prbs = [jnp.where(incl, big[C:, :LANES], 0.0).astype(prb_out.dtype) for big in bigs]
    vks = [jnp.where(same_head, _dot(v[rs, sl].astype(F32).T.astype(BF16), kh[rs, sl]), 0.0) for rs, sl in items]
    t_invs = [t.astype(BF16) for t in _pair_tri_inv([jnp.where(strict, big[:C, :LANES], 0.0) for big in bigs])]
    ws = [_dot(t, _pair_diag(at[rs, sl])).astype(w_out.dtype) for t, (rs, sl) in zip(t_invs, items)]
    uvs = [_dot(t, _pair_diag(g)) for t, g in zip(t_invs, gs)]

    def tile(parts):
        return jnp.concatenate(
            [jnp.concatenate(parts[npair * c:npair * (c + 1)], axis=1) for c in range(nchunk)], axis=0)

    yv_out[...] = tile(yvs)
    prb_out[...] = tile(prbs)
    w_out[...] = tile(ws)
    uv_out[...] = tile(uvs)
    vk_out[...] = jnp.concatenate(vks, axis=0)


def _c_state_kernel(w_ref, uv_ref, rt_ref, yv_ref, bh_ref, prb_ref, vk_ref, gam_ref, y_out, s_ref):
    @pl.when(pl.program_id(0) == 0)
    def _():
        s_ref[...] = jnp.zeros_like(s_ref)

    nb, C, D = w_ref.shape
    prow = lax.broadcasted_iota(jnp.int32, (LANES, LANES), 0)
    pcol = lax.broadcasted_iota(jnp.int32, (LANES, LANES), 1)
    same_head = (prow < HEAD) == (pcol < HEAD)
    npair = D // LANES
    items = [(b, p, slice(LANES * p, LANES * (p + 1))) for b in range(nb) for p in range(npair)]
    w, uv, rt, yv, bh, prb = w_ref[...], uv_ref[...], rt_ref[...], yv_ref[...], bh_ref[...], prb_ref[...]
    states = [s_ref[b * npair + p] for b, p, _ in items]
    urs = [_dot_nt(jnp.concatenate([w[b, :, sl], rt[b, :, sl]], axis=0), st.astype(BF16))
           for (b, _, sl), st in zip(items, states)]
    us = [ur[:C] + uv[b, :, sl] for ur, (b, _, sl) in zip(urs, items)]
    dss = [_dot(u.T.astype(BF16), bh[b, :, sl]) for u, (b, _, sl) in zip(us, items)]
    for (b, p, sl), st, ds in zip(items, states, dss):
        s_ref[b * npair + p] = (st * gam_ref[b, 0:1, sl] + jnp.where(same_head, ds, 0.0)
                                + vk_ref[b, LANES * p:LANES * (p + 1), :])
    pus = [_dot(prb[b, :, sl], _pair_diag(u)) for u, (b, _, sl) in zip(us, items)]
    ys = [ur[C:] + pu + yv[b, :, sl] for ur, pu, (b, _, sl) in zip(urs, pus, items)]
    for b in range(nb):
        y_out[b] = jnp.concatenate(ys[npair * b:npair * (b + 1)], axis=1)


def _c_out_kernel(y_ref, bonus_ref, g_ref, h_ref, lnw_ref, lnb_ref, wo_ref, out_ref):
    y = y_ref[...]
    mean = _seg_sum(y) * (1.0 / HEAD)
    yc = y - mean
    var = _seg_sum(yc * yc) * (1.0 / HEAD)
    yn = (yc * lax.rsqrt(var + GN_EPS)) * lnw_ref[...] + lnb_ref[...]
    out = ((yn + bonus_ref[...]) * g_ref[...]).astype(BF16)
    out_ref[...] = h_ref[...] + _dot(out, wo_ref[...])


def _rwkv_layer(h, norm_g, mu, w_rkv, w0, w1, w2, a0, a1, a2, g1, g2, k_k, k_a, r_k, ln_w, ln_b, w_out,
                *, batch, seq):
    T, D = h.shape
    C = CHUNK
    tm = 256
    nc = T // C
    npair = D // LANES
    row = lambda i: (i, 0)
    vec = lambda x: x.reshape(1, D)
    halo_spec = pl.BlockSpec((HALO_F32, D), lambda i: (jnp.maximum(i * (tm // HALO_F32) - 1, 0), 0))
    w_rkv = w_rkv.astype(BF16)
    consts = [vec(norm_g), mu, w_rkv[0], w_rkv[1], w_rkv[2], w1.astype(BF16), w2.astype(BF16), a1.astype(BF16),
              a2.astype(BF16), g1.astype(BF16), g2.astype(BF16), vec(w0), vec(a0), vec(k_k), vec(k_a), vec(r_k)]
    tile = pl.BlockSpec((tm, D), row)
    tile_shape = lambda dt: jax.ShapeDtypeStruct((T, D), dt)
    rt, bh, gam, gg, bonus, wq, uv, yv, prb, vk = pl.pallas_call(
        functools.partial(_c_in_kernel, tiles_per_seq=seq // tm),
        grid=(T // tm,),
        in_specs=[tile, halo_spec] + [_const_spec(c.shape) for c in consts],
        out_specs=[tile, tile, pl.BlockSpec((tm // C * HALO_F32, D), row), tile, tile, tile, tile, tile, tile,
                   pl.BlockSpec((tm // C * npair * LANES, LANES), row)],
        out_shape=[tile_shape(BF16), tile_shape(BF16), jax.ShapeDtypeStruct((nc * HALO_F32, D), F32),
                   tile_shape(F32), tile_shape(F32), tile_shape(BF16), tile_shape(F32), tile_shape(F32),
                   tile_shape(BF16), jax.ShapeDtypeStruct((nc * npair * LANES, LANES), F32)],
        compiler_params=_params("parallel"),
        name="c_in",
    )(h, h, *consts)

    cps = seq // C
    per_seq = lambda x: x.reshape(batch, x.shape[0] // batch, x.shape[1])
    cblk = lambda rows, cols: pl.BlockSpec((batch, rows, cols), lambda c: (0, c, 0))
    y = pl.pallas_call(
        _c_state_kernel,
        grid=(cps,),
        in_specs=[cblk(C, D)] * 6 + [cblk(npair * LANES, LANES), cblk(HALO_F32, D)],
        out_specs=cblk(C, D),
        out_shape=jax.ShapeDtypeStruct((batch, seq, D), F32),
        scratch_shapes=[pltpu.VMEM((batch * npair, LANES, LANES), F32)],
        compiler_params=_params("arbitrary"),
        name="c_state",
    )(*[per_seq(x) for x in (wq, uv, rt, yv, bh, prb, vk, gam)]).reshape(T, D)

    tmo = 512
    otile = pl.BlockSpec((tmo, D), row)
    return pl.pallas_call(
        _c_out_kernel,
        grid=(T // tmo,),
        in_specs=[otile] * 4 + [_const_spec((1, D))] * 2 + [_const_spec((D, D))],
        out_specs=otile,
        out_shape=jax.ShapeDtypeStruct((T, D), F32),
        compiler_params=_params("parallel"),
        name="c_out",
    )(y, bonus, gg, h, vec(ln_w), vec(ln_b), w_out.astype(BF16))


def _ffn_kernel(h_ref, hh_ref, g_ref, wg_ref, wu_ref, cwg_ref, cwu_ref, cbg_ref, cbu_ref, wd_ref,
                out_ref, hn_ref, act_ref, *, tiles_per_seq, fc):
    g = g_ref[...]
    halo = _rms(hh_ref[...], g)
    halo = jnp.where(pl.program_id(0) % tiles_per_seq == 0, 0.0, halo)
    hn_ref[:HALO_BF16, :] = halo.astype(BF16)
    hn_ref[HALO_BF16:, :] = _rms(h_ref[...], g).astype(BF16)

    def conv(u, cw, cb):
        y = cb + cw[2:3] * u + cw[1:2] * pltpu.roll(u, 1, axis=0) + cw[0:1] * pltpu.roll(u, 2, axis=0)
        return y[HALO_BF16:]

    for c in range(act_ref.shape[1] // fc):
        sl = slice(fc * c, fc * (c + 1))
        hn = hn_ref[...]
        yg = conv(_dot(hn, wg_ref[:, sl]), cwg_ref[:, sl], cbg_ref[:, sl])
        yu = conv(_dot(hn, wu_ref[:, sl]), cwu_ref[:, sl], cbu_ref[:, sl])
        act_ref[:, sl] = ((yg * jax.nn.sigmoid(yg)) * yu).astype(BF16)
    out_ref[...] = h_ref[...] + _dot(act_ref[...], wd_ref[...])


def _ffn_layer(h, norm_g, w_up, conv_w, conv_b, w_down, *, seq):
    T, D = h.shape
    F = w_down.shape[0]
    fc = 256
    tm = 512
    w_up = w_up.astype(BF16)
    consts = [norm_g.reshape(1, D), w_up[:, :F], w_up[:, F:], conv_w[:, :F], conv_w[:, F:],
              conv_b[None, :F], conv_b[None, F:], w_down.astype(BF16)]
    halo_spec = pl.BlockSpec((HALO_BF16, D), lambda i: (jnp.maximum(i * (tm // HALO_BF16) - 1, 0), 0))
    return pl.pallas_call(
        functools.partial(_ffn_kernel, tiles_per_seq=seq // tm, fc=fc),
        grid=(T // tm,),
        in_specs=[pl.BlockSpec((tm, D), lambda i: (i, 0)), halo_spec] + [_const_spec(c.shape) for c in consts],
        out_specs=pl.BlockSpec((tm, D), lambda i: (i, 0)),
        out_shape=jax.ShapeDtypeStruct((T, D), F32),
        scratch_shapes=[pltpu.VMEM((tm + HALO_BF16, D), BF16), pltpu.VMEM((tm, F), BF16)],
        compiler_params=_params("parallel"),
        name="ffn",
    )(h, h, *consts)


def _ple_kernel(h_ref, p_ref, g_ref, wg_ref, wp_ref, gf_ref, out_ref, *, final_norm):
    h = h_ref[...]
    hn = _rms(h, g_ref[...]).astype(BF16)
    gate = jax.nn.sigmoid(_dot(hn, wg_ref[...]))
    out = h + gate * _dot(p_ref[...].astype(BF16), wp_ref[...])
    if final_norm:
        out = _rms(out, gf_ref[...])
    out_ref[...] = out


def _ple_layer(h, p, layer, norm_g, w_gate, w_proj, norm_final, *, final_norm):
    T, D = h.shape
    P = p.shape[-1]
    tm = 512
    return pl.pallas_call(
        functools.partial(_ple_kernel, final_norm=final_norm),
        grid=(T // tm,),
        in_specs=[
            pl.BlockSpec((tm, D), lambda i: (i, 0)),
            pl.BlockSpec((None, tm, P), lambda i: (layer, i, 0)),
            _const_spec((1, D)),
            _const_spec((D, D)),
            _const_spec((P, D)),
            _const_spec((1, D)),
        ],
        out_specs=pl.BlockSpec((tm, D), lambda i: (i, 0)),
        out_shape=jax.ShapeDtypeStruct((T, D), F32),
        compiler_params=_params("parallel"),
        name="ple",
    )(h, p, norm_g.reshape(1, D), w_gate.astype(BF16), w_proj.astype(BF16), norm_final.reshape(1, D))


def _rotary_tables(positions):
    inv_freq = ROPE_THETA ** (-jnp.arange(0, ATT_HEAD_DIM, 2, dtype=F32) / ATT_HEAD_DIM)
    ang = positions.astype(F32).reshape(-1, 1) * inv_freq
    cos, sin = jnp.cos(ang), jnp.sin(ang)
    reps = LANES // ATT_HEAD_DIM
    cos_t = jnp.tile(jnp.concatenate([cos, cos], axis=1), (1, reps))
    sin_t = jnp.tile(jnp.concatenate([-sin, sin], axis=1), (1, reps))
    return cos_t, sin_t


def kernel(x, p, positions, norm_mix, norm_ffn, norm_ple, norm_final, a_w_in, a_conv_w, a_conv_b, a_gate_w, a_gate_b, a_lambda, a_w_out, b_w_qkv, b_w_out, c_mu, c_w_rkv, c_w0, c_w1, c_w2, c_a0, c_a1, c_a2, c_g1, c_g2, c_k_k, c_k_a, c_r_k, c_ln_w, c_ln_b, c_w_out, f_w_up, f_conv_w, f_conv_b, f_w_down, ple_w_proj, ple_w_gate):
    batch, seq, D = x.shape
    depth = norm_mix.shape[0]
    T = batch * seq
    h = x.reshape(T, D)
    pf = p.reshape(depth, T, p.shape[-1])
    cos_t, sin_t = _rotary_tables(positions)
    for i in range(depth):
        kind, j = i % 3, i // 3
        if kind == 0:
            h = _rglru_layer(h, norm_mix[i], a_w_in[j], a_conv_w[j], a_conv_b[j], a_gate_w[j], a_gate_b[j],
                             a_lambda[j], a_w_out[j], batch=batch, seq=seq)
        elif kind == 1:
            h = _attention_layer(h, norm_mix[i], cos_t, sin_t, b_w_qkv[j], b_w_out[j], batch=batch, seq=seq)
        else:
            h = _rwkv_layer(h, norm_mix[i], c_mu[j], c_w_rkv[j], c_w0[j], c_w1[j], c_w2[j], c_a0[j], c_a1[j],
                            c_a2[j], c_g1[j], c_g2[j], c_k_k[j], c_k_a[j], c_r_k[j], c_ln_w[j], c_ln_b[j],
                            c_w_out[j], batch=batch, seq=seq)
        h = _ffn_layer(h, norm_ffn[i], f_w_up[i], f_conv_w[i], f_conv_b[i], f_w_down[i], seq=seq)
        h = _ple_layer(h, pf, i, norm_ple[i], ple_w_gate[i], ple_w_proj[i], norm_final,
                       final_norm=(i == depth - 1))
    return h.reshape(batch, seq, D)
```

```python
import functools
import math

import jax
import jax.numpy as jnp
from jax import lax
from jax.experimental import pallas as pl
from jax.experimental.pallas import tpu as pltpu

F32 = jnp.float32
BF16 = jnp.bfloat16

RMS_EPS = 1e-6
LRU_HEADS = 4
LRU_CONV = 4
LRU_C = 8.0
ATT_HEAD_DIM = 64
DILATIONS = (1, 4, 16)
BAND = 128
ROPE_THETA = 10000.0
NEG_INF = -1e30
HEAD = 64
GN_EPS = 64e-5
CHUNK = 64

LANES = 128
GROUP = 128
HALO_BF16 = 16
HALO_F32 = 8
VMEM_LIMIT = 56 * 1024 * 1024


def _params(*sem):
    return pltpu.CompilerParams(dimension_semantics=sem, vmem_limit_bytes=VMEM_LIMIT)


def _const_spec(shape):
    nd = len(shape)
    return pl.BlockSpec(shape, lambda *_: (0,) * nd, pipeline_mode=pl.Buffered(1))


def _col_spec(rows, cols, j):
    return pl.BlockSpec((rows, cols), lambda *_: (0, j), pipeline_mode=pl.Buffered(1))


def _dot(a, b):
    return jnp.dot(a, b, preferred_element_type=F32)


def _dot_nt(a, b):
    return lax.dot_general(a, b, (((1,), (1,)), ((), ())), preferred_element_type=F32)


def _rms(x, g):
    ms = jnp.mean(x * x, axis=-1, keepdims=True)
    return (x * lax.rsqrt(ms + RMS_EPS)) * g


def _softplus(x):
    return jnp.maximum(x, 0.0) + jnp.log1p(jnp.exp(-jnp.abs(x)))


def _gelu_tanh(x):
    c = math.sqrt(2.0 / math.pi)
    return x * (0.5 * (1.0 + jnp.tanh(c * (x + 0.044715 * (x * x * x)))))


def _seg_sum(x):
    lane = lax.broadcasted_iota(jnp.int32, (1, LANES), 1)
    left = lane < HEAD
    outs = []
    for c in range(x.shape[1] // LANES):
        xc = x[:, LANES * c:LANES * (c + 1)]
        sl = jnp.sum(jnp.where(left, xc, 0.0), axis=-1, keepdims=True)
        sr = jnp.sum(jnp.where(left, 0.0, xc), axis=-1, keepdims=True)
        outs.append(jnp.where(left, sl, sr))
    return jnp.concatenate(outs, axis=1)


def _a_kernel(h_ref, g_ref, wg_ref, wx_ref, cw_ref, cb_ref, gw_ref, gb_ref, lam_ref, wo_ref,
              out_ref, cx_ref, ch_ref, *, tt):
    @pl.when(pl.program_id(1) == 0)
    def _():
        cx_ref[...] = jnp.zeros_like(cx_ref)
        ch_ref[...] = jnp.zeros_like(ch_ref)

    h = h_ref[...]
    hn = _rms(h, g_ref[...]).astype(BF16)
    y_gate = _gelu_tanh(_dot(hn, wg_ref[...]))
    x = _dot(hn, wx_ref[...])
    width = x.shape[1]
    blk = width // LRU_HEADS
    xe = jnp.concatenate([cx_ref[...], x], axis=0)
    cw = cw_ref[...]
    y = cb_ref[...] + cw[LRU_CONV - 1:LRU_CONV] * xe
    for k in range(1, LRU_CONV):
        y = y + cw[LRU_CONV - 1 - k:LRU_CONV - k] * pltpu.roll(xe, k, axis=0)
    y = y[HALO_F32:]
    cx_ref[...] = x[tt - HALO_F32:]

    yb = y.astype(BF16)
    rs, ins = [], []
    for hh in range(LRU_HEADS):
        gts = _dot(yb[:, blk * hh:blk * (hh + 1)], gw_ref[hh]) + gb_ref[:, 2 * blk * hh:2 * blk * (hh + 1)]
        rs.append(jax.nn.sigmoid(gts[:, :blk]))
        ins.append(jax.nn.sigmoid(gts[:, blk:]))
    r = jnp.concatenate(rs, axis=1)
    gi = jnp.concatenate(ins, axis=1)

    log_a = (-LRU_C * r) * _softplus(-lam_ref[...])
    a = jnp.exp(log_a)
    b = jnp.sqrt(-jnp.tanh(log_a) * (a * a + 1.0)) * (gi * y)

    row = lax.broadcasted_iota(jnp.int32, (HALO_F32, 1), 0)
    carry = ch_ref[0:1, :]
    groups = []
    for g0 in range(0, tt, HALO_F32):
        ag, bg = a[g0:g0 + HALO_F32], b[g0:g0 + HALO_F32]
        s = 1
        while s < HALO_F32:
            ok = row >= s
            bg = jnp.where(ok, ag * pltpu.roll(bg, s, axis=0) + bg, bg)
            ag = jnp.where(ok, ag * pltpu.roll(ag, s, axis=0), ag)
            s *= 2
        hg = bg + ag * carry
        carry = hg[HALO_F32 - 1:HALO_F32, :]
        groups.append(hg)
    hs = jnp.concatenate(groups, axis=0)
    ch_ref[...] = jnp.broadcast_to(carry, ch_ref.shape)

    out_ref[...] = h + _dot((hs * y_gate).astype(BF16), wo_ref[...])


def _rglru_layer(h, norm_g, w_in, conv_w, conv_b, gate_w, gate_b, lam, w_out, *, batch, seq):
    T, D = h.shape
    W = w_in.shape[1] // 2
    tt = 256
    nt = seq // tt
    blk = W // LRU_HEADS
    w_in = w_in.astype(BF16)
    tile = pl.BlockSpec((tt, D), lambda b, t: (b * nt + t, 0))
    return pl.pallas_call(
        functools.partial(_a_kernel, tt=tt),
        grid=(batch, nt),
        in_specs=[
            tile,
            _const_spec((1, D)),
            _col_spec(D, W, 0), _col_spec(D, W, 1),
            _const_spec((LRU_CONV, W)),
            _const_spec((1, W)),
            _const_spec((LRU_HEADS, blk, 2 * blk)),
            _const_spec((1, 2 * W)),
            _const_spec((1, W)),
            _const_spec((W, D)),
        ],
        out_specs=tile,
        out_shape=jax.ShapeDtypeStruct((T, D), F32),
        scratch_shapes=[pltpu.VMEM((HALO_F32, W), F32), pltpu.VMEM((HALO_F32, W), F32)],
        compiler_params=_params("parallel", "arbitrary"),
        name="a_mix",
    )(h, norm_g.reshape(1, D), w_in, w_in, conv_w, conv_b.reshape(1, W), gate_w.astype(BF16),
      gate_b.reshape(1, 2 * W), lam.reshape(1, W), w_out.astype(BF16))


def _b_qkv_kernel(h_ref, g_ref, w_ref, cos_ref, sin_ref, *refs):
    outs, zs_ref = refs[:-1], refs[-1]
    G = len(DILATIONS)
    tm = h_ref.shape[0]
    hn = _rms(h_ref[...], g_ref[...]).astype(BF16)
    cos = cos_ref[...]
    sin = sin_ref[...]
    lane = lax.broadcasted_iota(jnp.int32, (1, LANES), 1)
    first_half = (lane % ATT_HEAD_DIM) < (ATT_HEAD_DIM // 2)
    blk = outs[0].shape[-1]

    def emit(z, dst_ref, d):
        if d == 1:
            dst_ref[0] = z.astype(dst_ref.dtype)
            return
        for c in range(blk // LANES):
            zs_ref[c] = z[:, LANES * c:LANES * (c + 1)]
        for r in range(d):
            for c in range(blk // LANES):
                part = zs_ref.at[c][pl.ds(r, tm // d, stride=d), :]
                dst_ref[r, :, LANES * c:LANES * (c + 1)] = part.astype(dst_ref.dtype)

    for j in range(2 * G + 1):
        z = _dot(hn, w_ref[:, blk * j:blk * (j + 1)])
        if j < 2 * G:
            cols = []
            for c in range(blk // LANES):
                zc = z[:, LANES * c:LANES * (c + 1)]
                up = pltpu.roll(zc, LANES - ATT_HEAD_DIM // 2, axis=1)
                dn = pltpu.roll(zc, ATT_HEAD_DIM // 2, axis=1)
                oc = zc * cos + jnp.where(first_half, up, dn) * sin
                if j < G:
                    oc = oc * (ATT_HEAD_DIM ** -0.5)
                cols.append(oc)
            z = jnp.concatenate(cols, axis=1)
            emit(z, outs[j], DILATIONS[j % G])
        else:
            for g in range(G):
                emit(z, outs[2 * G + g], DILATIONS[g])


def _b_attn_kernel(q_ref, kp_ref, kc_ref, vp_ref, vc_ref, o_ref, l_ref):
    n = pl.program_id(2)
    row = lax.broadcasted_iota(jnp.int32, (BAND, 2 * BAND), 0)
    col = lax.broadcasted_iota(jnp.int32, (BAND, 2 * BAND), 1)
    dist = BAND + row - col
    valid = (dist >= 0) & (dist <= BAND) & ((col >= BAND) | (n > 0))
    lane = lax.broadcasted_iota(jnp.int32, (1, LANES), 1)
    left = lane < ATT_HEAD_DIM
    zero = jnp.zeros((), BF16)
    pairs = [slice(LANES * p, LANES * (p + 1)) for p in range(q_ref.shape[1] // LANES)]
    q, kp, kc, vp, vc = q_ref[...], kp_ref[...], kc_ref[...], vp_ref[...], vc_ref[...]
    ks = [jnp.concatenate([kp[:, sl], kc[:, sl]], axis=0) for sl in pairs]
    vs = [jnp.concatenate([vp[:, sl], vc[:, sl]], axis=0) for sl in pairs]
    heads = [(p, hh) for p in range(len(pairs)) for hh in range(2)]
    ss = [jnp.where(valid, _dot_nt(jnp.where(left if hh == 0 else ~left, q[:, pairs[p]], zero), ks[p]), NEG_INF)
          for p, hh in heads]
    ms = [jnp.max(s, axis=-1, keepdims=True) for s in ss]
    es = [jnp.exp(s - m) for s, m in zip(ss, ms)]
    dens = [jnp.sum(e, axis=-1, keepdims=True) for e in es]
    pvs = [_dot(e.astype(BF16), vs[p]) for e, (p, _) in zip(es, heads)]
    outs = [pv / den for pv, den in zip(pvs, dens)]
    lses = [m + jnp.log(den) for m, den in zip(ms, dens)]
    o_ref[...] = jnp.concatenate(
        [jnp.where(left, outs[2 * p], outs[2 * p + 1]) for p in range(len(pairs))], axis=1).astype(o_ref.dtype)
    l_ref[...] = jnp.concatenate(
        [jnp.where(left, lses[2 * p], lses[2 * p + 1]) for p in range(len(pairs))], axis=1)


def _b_out_kernel(o1_ref, o2_ref, o3_ref, l1_ref, l2_ref, l3_ref, h_ref, wo_ref, out_ref, *scratch):
    def rows(ref, scr):
        d, n = ref.shape[0], ref.shape[1]
        if d == 1:
            return ref[0].astype(F32)
        ncol = ref.shape[2] // LANES
        for r in range(d):
            part = ref[r].astype(F32)
            for c in range(ncol):
                scr.at[c][pl.ds(r, n, stride=d), :] = part[:, LANES * c:LANES * (c + 1)]
        return jnp.concatenate([scr[c] for c in range(ncol)], axis=1)

    l1, l2, l3 = rows(l1_ref, None), rows(l2_ref, scratch[0]), rows(l3_ref, scratch[1])
    m = jnp.maximum(jnp.maximum(l1, l2), l3)
    e1, e2, e3 = jnp.exp(l1 - m), jnp.exp(l2 - m), jnp.exp(l3 - m)
    o = e1 * rows(o1_ref, None) + e2 * rows(o2_ref, scratch[2]) + e3 * rows(o3_ref, scratch[3])
    o = o / (e1 + e2 + e3)
    out_ref[...] = h_ref[...] + _dot(o.astype(BF16), wo_ref[...])


def _attention_layer(h, norm_g, cos_t, sin_t, w_qkv, w_out, *, batch, seq):
    T, D = h.shape
    HD = w_out.shape[0]
    NQ = w_qkv.shape[1]
    G = len(DILATIONS)
    tm = 512
    tps = seq // tm

    def split_spec(d):
        return pl.BlockSpec((None, d, tm // d, HD), lambda i: (i // tps, 0, i % tps, 0))

    def split_shape(d, dtype):
        return jax.ShapeDtypeStruct((batch, d, seq // d, HD), dtype)

    qkv = pl.pallas_call(
        _b_qkv_kernel,
        grid=(T // tm,),
        in_specs=[
            pl.BlockSpec((tm, D), lambda i: (i, 0)),
            _const_spec((1, D)),
            _const_spec((D, NQ)),
            pl.BlockSpec((tm, LANES), lambda i: (i, 0)),
            pl.BlockSpec((tm, LANES), lambda i: (i, 0)),
        ],
        out_specs=[split_spec(d) for _ in range(3) for d in DILATIONS],
        out_shape=[split_shape(d, BF16) for _ in range(3) for d in DILATIONS],
        scratch_shapes=[pltpu.VMEM((HD // LANES, tm, LANES), F32)],
        compiler_params=_params("parallel"),
        name="b_qkv",
    )(h, norm_g.reshape(1, D), w_qkv.astype(BF16), cos_t, sin_t)

    outs, lses = [], []
    blk = (None, None, BAND, HD)
    cur = lambda b, r, n: (b, r, n, 0)
    prev = lambda b, r, n: (b, r, jnp.maximum(n - 1, 0), 0)
    for g, d in enumerate(DILATIONS):
        q_g, k_g, v_g = qkv[g], qkv[G + g], qkv[2 * G + g]
        o_g, l_g = pl.pallas_call(
            _b_attn_kernel,
            grid=(batch, d, seq // d // BAND),
            in_specs=[pl.BlockSpec(blk, cur), pl.BlockSpec(blk, prev), pl.BlockSpec(blk, cur),
                      pl.BlockSpec(blk, prev), pl.BlockSpec(blk, cur)],
            out_specs=[pl.BlockSpec(blk, cur), pl.BlockSpec(blk, cur)],
            out_shape=[split_shape(d, BF16), split_shape(d, F32)],
            compiler_params=_params("parallel", "parallel", "arbitrary"),
            name=f"b_attn_d{d}",
        )(q_g, k_g, k_g, v_g, v_g)
        outs.append(o_g)
        lses.append(l_g)

    row = lambda i: (i, 0)
    return pl.pallas_call(
        _b_out_kernel,
        grid=(T // tm,),
        in_specs=[split_spec(d) for _ in range(2) for d in DILATIONS]
        + [pl.BlockSpec((tm, D), row), _const_spec((HD, D))],
        out_specs=pl.BlockSpec((tm, D), row),
        out_shape=jax.ShapeDtypeStruct((T, D), F32),
        scratch_shapes=[pltpu.VMEM((HD // LANES, tm, LANES), F32)] * 4,
        compiler_params=_params("parallel"),
        name="b_out",
    )(*outs, *lses, h, w_out.astype(BF16))


def _c_in_kernel(h_ref, hh_ref, g_ref, mu_ref, wr_ref, wk_ref, wv_ref, w1_ref, w2_ref, a1_ref, a2_ref,
                 g1_ref, g2_ref, w0_ref, a0_ref, kk_ref, ka_ref, rk_ref,
                 rt_out, bh_out, gam_out, g_out, bonus_out, w_out, uv_out, yv_out, prb_out, vk_out,
                 *, tiles_per_seq):
    g = g_ref[...]
    hn = _rms(h_ref[...], g)
    halo = _rms(hh_ref[...], g)
    halo = jnp.where(pl.program_id(0) % tiles_per_seq == 0, 0.0, halo)
    he = jnp.concatenate([halo, hn], axis=0)
    xx = pltpu.roll(he, 1, axis=0)[HALO_F32:] - hn
    mu = mu_ref[...]

    def mix(c):
        return (hn + xx * mu[c:c + 1]).astype(BF16)

    r = _dot(mix(0), wr_ref[...])
    k = _dot(mix(1), wk_ref[...])
    v = _dot(mix(2), wv_ref[...])
    wl = _dot(jnp.tanh(_dot(mix(3), w1_ref[...])).astype(BF16), w2_ref[...])
    al = _dot(_dot(mix(4), a1_ref[...]).astype(BF16), a2_ref[...])
    g_out[...] = _dot(jax.nn.sigmoid(_dot(mix(5), g1_ref[...])).astype(BF16), g2_ref[...])

    z = -(w0_ref[...] + wl)
    w = -(jnp.maximum(z, 0.0) + jnp.log(1.0 + jnp.exp(-jnp.abs(z)))) - 0.5
    lw = -jnp.exp(w)
    a = jax.nn.sigmoid(a0_ref[...] + al)
    kk = k * kk_ref[...]
    kk = kk / jnp.maximum(jnp.sqrt(_seg_sum(kk * kk)), 1e-12)
    kh = k * (1.0 + (a - 1.0) * ka_ref[...])
    bb = kk * a
    bonus_out[...] = _seg_sum(r * kh * rk_ref[...]) * v

    tm = lw.shape[0]
    rowc = lax.broadcasted_iota(jnp.int32, (tm, 1), 0) % CHUNK
    cum = lw
    s = 1
    while s < CHUNK:
        cum = cum + jnp.where(rowc >= s, pltpu.roll(cum, s, axis=0), 0.0)
        s *= 2
    lasts = [jnp.broadcast_to(cum[CHUNK * (c + 1) - 1:CHUNK * (c + 1), :], (CHUNK, cum.shape[1]))
             for c in range(tm // CHUNK)]
    last = jnp.concatenate(lasts, axis=0)
    e_neg = jnp.exp(-cum)
    e_end = jnp.exp(last - cum)
    at = (-kk * jnp.exp(cum - lw)).astype(BF16)
    bt = (bb * e_neg).astype(BF16)
    kt = (kh * e_neg).astype(BF16)
    rt = (r * jnp.exp(cum)).astype(BF16)
    rt_out[...] = rt
    bh_out[...] = (bb * e_end).astype(bh_out.dtype)
    for c in range(tm // CHUNK):
        gam_out[HALO_F32 * c:HALO_F32 * (c + 1), :] = jnp.exp(lasts[c][:HALO_F32])
    _chunk_algebra(at, bt, kt, rt, (kh * e_end).astype(BF16), v.astype(BF16), w_out, uv_out, yv_out, prb_out, vk_out)


def _head_masks(shape):
    head = lax.broadcasted_iota(jnp.int32, shape, len(shape) - 1) // HEAD
    return [head == h for h in range(GROUP // HEAD)]


def _group_diag(x, dtype=BF16):
    x = x.astype(dtype)
    zero = jnp.zeros((), dtype)
    return jnp.concatenate([jnp.where(m, x, zero) for m in _head_masks((1, GROUP))], axis=0)


def _group_tri_inv(ms):
    c = ms[0].shape[0]
    row = lax.broadcasted_iota(jnp.int32, ms[0].shape, 0)
    col = lax.broadcasted_iota(jnp.int32, ms[0].shape, 1) % HEAD
    eye = jnp.where(row == col, 1.0, 0.0)
    first = (row >> 1 == col >> 1) & (row > col)
    xs = [eye + jnp.where(first, m, 0.0) for m in ms]
    mbs = [m.astype(BF16) for m in ms]
    heads = _head_masks(ms[0].shape)
    zero = jnp.zeros((), BF16)
    k = 1
    while (1 << k) < c:
        off = (row >> (k + 1) == col >> (k + 1)) & (((row >> k) & 1) == 1) & (((col >> k) & 1) == 0)
        offs = [off & hm for hm in heads]
        zs = [_dot(x.astype(BF16), jnp.concatenate([jnp.where(o, mb, zero) for o in offs], axis=0))
              for x, mb in zip(xs, mbs)]
        xs = [x + _dot(z.astype(BF16), _group_diag(x)) for x, z in zip(xs, zs)]
        k += 1
    return xs


def _chunk_algebra(at, bt, kt, rt, kh, v, w_out, uv_out, yv_out, prb_out, vk_out):
    C = CHUNK
    row = lax.broadcasted_iota(jnp.int32, (C, GROUP), 0)
    col = lax.broadcasted_iota(jnp.int32, (C, GROUP), 1) % HEAD
    strict = row > col
    incl = row >= col
    heads = _head_masks((1, GROUP))
    zero = jnp.zeros((), BF16)
    ngrp = at.shape[1] // GROUP
    nchunk = at.shape[0] // C
    items = [(slice(C * c, C * (c + 1)), slice(GROUP * p, GROUP * (p + 1))) for c in range(nchunk) for p in range(ngrp)]
    bigs = []
    for rs, sl in items:
        lhs = jnp.concatenate([at[rs, sl], rt[rs, sl]], axis=0)
        rhs = jnp.concatenate([jnp.where(m, bt[rs, sl], zero) for m in heads]
                              + [jnp.where(m, kt[rs, sl], zero) for m in heads], axis=0)
        bigs.append(_dot_nt(lhs, rhs))
    v_bds = [_group_diag(v[rs, sl]) for rs, sl in items]
    gs = [_dot(jnp.where(strict, big[:C, GROUP:], 0.0).astype(BF16), v_bd) for big, v_bd in zip(bigs, v_bds)]
    yvs = [_dot(jnp.where(incl, big[C:, GROUP:], 0.0).astype(BF16), v_bd) for big, v_bd in zip(bigs, v_bds)]
    prbs = [jnp.where(incl, big[C:, :GROUP], 0.0).astype(prb_out.dtype) for big in bigs]
    vks = []
    for rs, sl in items:
        full = _dot(v[rs, sl].astype(F32).T.astype(BF16), kh[rs, sl])
        vks.append(sum(jnp.where(m, full[HEAD * h:HEAD * (h + 1)], 0.0) for h, m in enumerate(heads)))
    t_invs = [t.astype(BF16) for t in _group_tri_inv([jnp.where(strict, big[:C, :GROUP], 0.0) for big in bigs])]
    ws = [_dot(t, _group_diag(at[rs, sl])).astype(w_out.dtype) for t, (rs, sl) in zip(t_invs, items)]
    uvs = [_dot(t, _group_diag(g)) for t, g in zip(t_invs, gs)]

    def tile(parts):
        return jnp.concatenate(
            [jnp.concatenate(parts[ngrp * c:ngrp * (c + 1)], axis=1) for c in range(nchunk)], axis=0)

    yv_out[...] = tile(yvs)
    prb_out[...] = tile(prbs)
    w_out[...] = tile(ws)
    uv_out[...] = tile(uvs)
    vk_out[...] = tile(vks)


def _c_state_kernel(w_ref, uv_ref, rt_ref, yv_ref, bh_ref, prb_ref, vk_ref, gam_ref, bonus_ref, g_ref, h_ref,
                    lnw_ref, lnb_ref, wo_ref, out_ref, s_ref):
    @pl.when(pl.program_id(0) == 0)
    def _():
        s_ref[...] = jnp.zeros_like(s_ref)

    nb, C, D = w_ref.shape
    prow = lax.broadcasted_iota(jnp.int32, (GROUP, GROUP), 0) // HEAD
    pcol = lax.broadcasted_iota(jnp.int32, (GROUP, GROUP), 1) // HEAD
    same_head = prow == pcol
    ngrp = D // GROUP
    items = [(b, p, slice(GROUP * p, GROUP * (p + 1))) for b in range(nb) for p in range(ngrp)]
    w, uv, rt, yv, bh, prb, vk = (w_ref[...], uv_ref[...], rt_ref[...], yv_ref[...], bh_ref[...], prb_ref[...],
                                  vk_ref[...])
    states = [s_ref[b * ngrp + p] for b, p, _ in items]
    urs = [_dot_nt(jnp.concatenate([w[b, :, sl], rt[b, :, sl]], axis=0), st.astype(BF16))
           for (b, _, sl), st in zip(items, states)]
    us = [ur[:C] + uv[b, :, sl] for ur, (b, _, sl) in zip(urs, items)]
    dss = [_dot(u.T.astype(BF16), bh[b, :, sl]) for u, (b, _, sl) in zip(us, items)]
    for (b, p, sl), st, ds in zip(items, states, dss):
        s_ref[b * ngrp + p] = (st * gam_ref[b, 0:1, sl] + jnp.where(same_head, ds, 0.0)
                               + _group_diag(vk[b, :, sl], F32))
    pus = [_dot(prb[b, :, sl], _group_diag(u)) for u, (b, _, sl) in zip(us, items)]
    ys = [ur[C:] + pu + yv[b, :, sl] for ur, pu, (b, _, sl) in zip(urs, pus, items)]

    y = jnp.concatenate([jnp.concatenate(ys[ngrp * b:ngrp * (b + 1)], axis=1) for b in range(nb)], axis=0)
    mean = _seg_sum(y) * (1.0 / HEAD)
    yc = y - mean
    var = _seg_sum(yc * yc) * (1.0 / HEAD)
    yn = (yc * lax.rsqrt(var + GN_EPS)) * lnw_ref[...] + lnb_ref[...]
    gated = ((yn + bonus_ref[...].reshape(nb * C, D)) * g_ref[...].reshape(nb * C, D)).astype(BF16)
    out_ref[...] = h_ref[...] + _dot(gated, wo_ref[...]).reshape(nb, C, D)


def _rwkv_layer(h, norm_g, mu, w_rkv, w0, w1, w2, a0, a1, a2, g1, g2, k_k, k_a, r_k, ln_w, ln_b, w_out,
                *, batch, seq):
    T, D = h.shape
    C = CHUNK
    tm = 256
    nc = T // C
    npair = D // LANES
    row = lambda i: (i, 0)
    vec = lambda x: x.reshape(1, D)
    halo_spec = pl.BlockSpec((HALO_F32, D), lambda i: (jnp.maximum(i * (tm // HALO_F32) - 1, 0), 0))
    w_rkv = w_rkv.astype(BF16)
    consts = [vec(norm_g), mu, w_rkv, w_rkv, w_rkv, w1.astype(BF16), w2.astype(BF16), a1.astype(BF16),
              a2.astype(BF16), g1.astype(BF16), g2.astype(BF16), vec(w0), vec(a0), vec(k_k), vec(k_a), vec(r_k)]
    mat_spec = lambda c: pl.BlockSpec((None, D, D), lambda *_: (c, 0, 0), pipeline_mode=pl.Buffered(1))
    const_specs = [mat_spec(i - 2) if 2 <= i <= 4 else _const_spec(c.shape) for i, c in enumerate(consts)]
    tile = pl.BlockSpec((tm, D), row)
    tile_shape = lambda dt: jax.ShapeDtypeStruct((T, D), dt)
    rt, bh, gam, gg, bonus, wq, uv, yv, prb, vk = pl.pallas_call(
        functools.partial(_c_in_kernel, tiles_per_seq=seq // tm),
        grid=(T // tm,),
        in_specs=[tile, halo_spec] + const_specs,
        out_specs=[tile, tile, pl.BlockSpec((tm // C * HALO_F32, D), row), tile, tile, tile, tile, tile, tile, tile],
        out_shape=[tile_shape(BF16), tile_shape(BF16), jax.ShapeDtypeStruct((nc * HALO_F32, D), F32),
                   tile_shape(F32), tile_shape(F32), tile_shape(BF16), tile_shape(F32), tile_shape(F32),
                   tile_shape(BF16), tile_shape(F32)],
        compiler_params=_params("parallel"),
        name="c_in",
    )(h, h, *consts)

    cps = seq // C
    per_seq = lambda x: x.reshape(batch, x.shape[0] // batch, x.shape[1])
    cblk = lambda rows, cols: pl.BlockSpec((batch, rows, cols), lambda c: (0, c, 0))
    return pl.pallas_call(
        _c_state_kernel,
        grid=(cps,),
        in_specs=[cblk(C, D)] * 7 + [cblk(HALO_F32, D)] + [cblk(C, D)] * 3
        + [_const_spec((1, D))] * 2 + [_const_spec((D, D))],
        out_specs=cblk(C, D),
        out_shape=jax.ShapeDtypeStruct((batch, seq, D), F32),
        scratch_shapes=[pltpu.VMEM((batch * (D // GROUP), GROUP, GROUP), F32)],
        compiler_params=_params("arbitrary"),
        name="c_state",
    )(*[per_seq(x) for x in (wq, uv, rt, yv, bh, prb, vk, gam, bonus, gg, h)],
      vec(ln_w), vec(ln_b), w_out.astype(BF16)).reshape(T, D)


def _ffn_ple_kernel(h_ref, hh_ref, p_ref, g_ref, wg_ref, wu_ref, cwg_ref, cwu_ref, cbg_ref, cbu_ref, wd_ref,
                    gp_ref, wpg_ref, wpp_ref, gf_ref, out_ref, hn_ref, act_ref, *, tiles_per_seq, fc, final_norm):
    g = g_ref[...]
    h = h_ref[...]
    halo = _rms(hh_ref[...], g)
    halo = jnp.where(pl.program_id(0) % tiles_per_seq == 0, 0.0, halo)
    hn_ref[:HALO_BF16, :] = halo.astype(BF16)
    hn_ref[HALO_BF16:, :] = _rms(h, g).astype(BF16)

    def conv(u, cw, cb):
        y = cb + cw[2:3] * u + cw[1:2] * pltpu.roll(u, 1, axis=0) + cw[0:1] * pltpu.roll(u, 2, axis=0)
        return y[HALO_BF16:]

    for c in range(act_ref.shape[1] // fc):
        sl = slice(fc * c, fc * (c + 1))
        hn = hn_ref[...]
        yg = conv(_dot(hn, wg_ref[:, sl]), cwg_ref[:, sl], cbg_ref[:, sl])
        yu = conv(_dot(hn, wu_ref[:, sl]), cwu_ref[:, sl], cbu_ref[:, sl])
        act_ref[:, sl] = ((yg * jax.nn.sigmoid(yg)) * yu).astype(BF16)
    h = h + _dot(act_ref[...], wd_ref[...])

    gate = jax.nn.sigmoid(_dot(_rms(h, gp_ref[...]).astype(BF16), wpg_ref[...]))
    h = h + gate * _dot(p_ref[...].astype(BF16), wpp_ref[...])
    if final_norm:
        h = _rms(h, gf_ref[...])
    out_ref[...] = h


def _ffn_ple_layer(h, p, layer, norm_ffn, w_up, conv_w, conv_b, w_down, norm_ple, w_gate, w_proj, norm_final,
                   *, seq, final_norm):
    T, D = h.shape
    F = w_down.shape[0]
    P = p.shape[-1]
    fc = 256
    tm = 512
    vec = lambda x: x.reshape(1, -1)
    halo_spec = pl.BlockSpec((HALO_BF16, D), lambda i: (jnp.maximum(i * (tm // HALO_BF16) - 1, 0), 0))
    kc = conv_w.shape[0]
    return pl.pallas_call(
        functools.partial(_ffn_ple_kernel, tiles_per_seq=seq // tm, fc=fc, final_norm=final_norm),
        grid=(T // tm,),
        in_specs=[
            pl.BlockSpec((tm, D), lambda i: (i, 0)), halo_spec,
            pl.BlockSpec((None, tm, P), lambda i: (layer, i, 0)),
            _const_spec((1, D)),
            _col_spec(D, F, 0), _col_spec(D, F, 1),
            _col_spec(kc, F, 0), _col_spec(kc, F, 1),
            _col_spec(1, F, 0), _col_spec(1, F, 1),
            _const_spec((F, D)),
            _const_spec((1, D)), _const_spec((D, D)), _const_spec((P, D)), _const_spec((1, D)),
        ],
        out_specs=pl.BlockSpec((tm, D), lambda i: (i, 0)),
        out_shape=jax.ShapeDtypeStruct((T, D), F32),
        scratch_shapes=[pltpu.VMEM((tm + HALO_BF16, D), BF16), pltpu.VMEM((tm, F), BF16)],
        compiler_params=_params("parallel"),
        name="ffn_ple",
    )(h, h, p, vec(norm_ffn), w_up.astype(BF16), w_up.astype(BF16), conv_w, conv_w, vec(conv_b), vec(conv_b),
      w_down.astype(BF16), vec(norm_ple), w_gate.astype(BF16), w_proj.astype(BF16), vec(norm_final))


def _rotary_tables(positions):
    inv_freq = ROPE_THETA ** (-jnp.arange(0, ATT_HEAD_DIM, 2, dtype=F32) / ATT_HEAD_DIM)
    ang = positions.astype(F32).reshape(-1, 1) * inv_freq
    cos, sin = jnp.cos(ang), jnp.sin(ang)
    reps = LANES // ATT_HEAD_DIM
    cos_t = jnp.tile(jnp.concatenate([cos, cos], axis=1), (1, reps))
    sin_t = jnp.tile(jnp.concatenate([-sin, sin], axis=1), (1, reps))
    return cos_t, sin_t


def kernel(x, p, positions, norm_mix, norm_ffn, norm_ple, norm_final, a_w_in, a_conv_w, a_conv_b, a_gate_w, a_gate_b, a_lambda, a_w_out, b_w_qkv, b_w_out, c_mu, c_w_rkv, c_w0, c_w1, c_w2, c_a0, c_a1, c_a2, c_g1, c_g2, c_k_k, c_k_a, c_r_k, c_ln_w, c_ln_b, c_w_out, f_w_up, f_conv_w, f_conv_b, f_w_down, ple_w_proj, ple_w_gate):
    batch, seq, D = x.shape
    depth = norm_mix.shape[0]
    T = batch * seq
    h = x.reshape(T, D)
    pf = p.reshape(depth, T, p.shape[-1])
    cos_t, sin_t = _rotary_tables(positions)
    for i in range(depth):
        kind, j = i % 3, i // 3
        if kind == 0:
            h = _rglru_layer(h, norm_mix[i], a_w_in[j], a_conv_w[j], a_conv_b[j], a_gate_w[j], a_gate_b[j],
                             a_lambda[j], a_w_out[j], batch=batch, seq=seq)
        elif kind == 1:
            h = _attention_layer(h, norm_mix[i], cos_t, sin_t, b_w_qkv[j], b_w_out[j], batch=batch, seq=seq)
        else:
            h = _rwkv_layer(h, norm_mix[i], c_mu[j], c_w_rkv[j], c_w0[j], c_w1[j], c_w2[j], c_a0[j], c_a1[j],
                            c_a2[j], c_g1[j], c_g2[j], c_k_k[j], c_k_a[j], c_r_k[j], c_ln_w[j], c_ln_b[j],
                            c_w_out[j], batch=batch, seq=seq)
        h = _ffn_ple_layer(h, pf, i, norm_ffn[i], f_w_up[i], f_conv_w[i], f_conv_b[i], f_w_down[i], norm_ple[i],
                           ple_w_gate[i], ple_w_proj[i], norm_final, seq=seq, final_norm=(i == depth - 1))
    return h.reshape(batch, seq, D)
```

```python
import functools
import math

import jax
import jax.numpy as jnp
from jax import lax
from jax.experimental import pallas as pl
from jax.experimental.pallas import tpu as pltpu

F32 = jnp.float32
BF16 = jnp.bfloat16

RMS_EPS = 1e-6
LRU_HEADS = 4
LRU_CONV = 4
LRU_C = 8.0
ATT_HEAD_DIM = 64
DILATIONS = (1, 4, 16)
BAND = 128
ROPE_THETA = 10000.0
NEG_INF = -1e30
HEAD = 64
GN_EPS = 64e-5
CHUNK = 64

LANES = 128
GROUP = 128
HALO_BF16 = 16
HALO_F32 = 8
VMEM_LIMIT = 56 * 1024 * 1024


def _params(*sem):
    return pltpu.CompilerParams(dimension_semantics=sem, vmem_limit_bytes=VMEM_LIMIT)


def _const_spec(shape):
    nd = len(shape)
    return pl.BlockSpec(shape, lambda *_: (0,) * nd, pipeline_mode=pl.Buffered(1))


def _layer_spec(layer, rows, cols, j=0):
    return pl.BlockSpec((None, rows, cols), lambda *_: (layer, 0, j), pipeline_mode=pl.Buffered(1))


def _rows3(x):
    return x.reshape(x.shape[0], 1, x.shape[1])


def _dot(a, b):
    return jnp.dot(a, b, preferred_element_type=F32)


def _dot_nt(a, b):
    return lax.dot_general(a, b, (((1,), (1,)), ((), ())), preferred_element_type=F32)


def _rms(x, g):
    ms = jnp.mean(x * x, axis=-1, keepdims=True)
    return (x * lax.rsqrt(ms + RMS_EPS)) * g


def _softplus(x):
    return jnp.maximum(x, 0.0) + jnp.log1p(jnp.exp(-jnp.abs(x)))


def _gelu_tanh(x):
    c = math.sqrt(2.0 / math.pi)
    return x * (0.5 * (1.0 + jnp.tanh(c * (x + 0.044715 * (x * x * x)))))


def _seg_sum(x):
    lane = lax.broadcasted_iota(jnp.int32, (1, LANES), 1)
    left = lane < HEAD
    outs = []
    for c in range(x.shape[1] // LANES):
        xc = x[:, LANES * c:LANES * (c + 1)]
        sl = jnp.sum(jnp.where(left, xc, 0.0), axis=-1, keepdims=True)
        sr = jnp.sum(jnp.where(left, 0.0, xc), axis=-1, keepdims=True)
        outs.append(jnp.where(left, sl, sr))
    return jnp.concatenate(outs, axis=1)


def _a_kernel(h_ref, g_ref, wg_ref, wx_ref, cw_ref, cb_ref, gw_ref, gb_ref, lam_ref, wo_ref,
              out_ref, cx_ref, ch_ref, *, tt):
    @pl.when(pl.program_id(1) == 0)
    def _():
        cx_ref[...] = jnp.zeros_like(cx_ref)
        ch_ref[...] = jnp.zeros_like(ch_ref)

    h = h_ref[...]
    hn = _rms(h, g_ref[...]).astype(BF16)
    y_gate = _gelu_tanh(_dot(hn, wg_ref[...]))
    x = _dot(hn, wx_ref[...])
    width = x.shape[1]
    blk = width // LRU_HEADS
    xe = jnp.concatenate([cx_ref[...], x], axis=0)
    cw = cw_ref[...]
    y = cb_ref[...] + cw[LRU_CONV - 1:LRU_CONV] * xe
    for k in range(1, LRU_CONV):
        y = y + cw[LRU_CONV - 1 - k:LRU_CONV - k] * pltpu.roll(xe, k, axis=0)
    y = y[HALO_F32:]
    cx_ref[...] = x[tt - HALO_F32:]

    yb = y.astype(BF16)
    rs, ins = [], []
    for hh in range(LRU_HEADS):
        gts = _dot(yb[:, blk * hh:blk * (hh + 1)], gw_ref[hh]) + gb_ref[:, 2 * blk * hh:2 * blk * (hh + 1)]
        rs.append(jax.nn.sigmoid(gts[:, :blk]))
        ins.append(jax.nn.sigmoid(gts[:, blk:]))
    r = jnp.concatenate(rs, axis=1)
    gi = jnp.concatenate(ins, axis=1)

    log_a = (-LRU_C * r) * _softplus(-lam_ref[...])
    a = jnp.exp(log_a)
    b = jnp.sqrt(-jnp.tanh(log_a) * (a * a + 1.0)) * (gi * y)

    row = lax.broadcasted_iota(jnp.int32, (HALO_F32, 1), 0)
    carry = ch_ref[0:1, :]
    groups = []
    for g0 in range(0, tt, HALO_F32):
        ag, bg = a[g0:g0 + HALO_F32], b[g0:g0 + HALO_F32]
        s = 1
        while s < HALO_F32:
            ok = row >= s
            bg = jnp.where(ok, ag * pltpu.roll(bg, s, axis=0) + bg, bg)
            ag = jnp.where(ok, ag * pltpu.roll(ag, s, axis=0), ag)
            s *= 2
        hg = bg + ag * carry
        carry = hg[HALO_F32 - 1:HALO_F32, :]
        groups.append(hg)
    hs = jnp.concatenate(groups, axis=0)
    ch_ref[...] = jnp.broadcast_to(carry, ch_ref.shape)

    out_ref[...] = h + _dot((hs * y_gate).astype(BF16), wo_ref[...])


def _rglru_layer(h, j, norm_g, w_in, conv_w, conv_b, gate_w, gate_b, lam, w_out, *, batch, seq):
    T, D = h.shape
    W = w_in.shape[2] // 2
    tt = 256
    nt = seq // tt
    blk = W // LRU_HEADS
    tile = pl.BlockSpec((tt, D), lambda b, t: (b * nt + t, 0))
    return pl.pallas_call(
        functools.partial(_a_kernel, tt=tt),
        grid=(batch, nt),
        in_specs=[
            tile,
            _const_spec((1, D)),
            _layer_spec(j, D, W, 0), _layer_spec(j, D, W, 1),
            _layer_spec(j, LRU_CONV, W),
            _layer_spec(j, 1, W),
            pl.BlockSpec((None, LRU_HEADS, blk, 2 * blk), lambda *_: (j, 0, 0, 0), pipeline_mode=pl.Buffered(1)),
            _layer_spec(j, 1, 2 * W),
            _layer_spec(j, 1, W),
            _layer_spec(j, W, D),
        ],
        out_specs=tile,
        out_shape=jax.ShapeDtypeStruct((T, D), F32),
        scratch_shapes=[pltpu.VMEM((HALO_F32, W), F32), pltpu.VMEM((HALO_F32, W), F32)],
        compiler_params=_params("parallel", "arbitrary"),
        name="a_mix",
    )(h, norm_g.reshape(1, D), w_in, w_in, conv_w, _rows3(conv_b), gate_w,
      gate_b.reshape(gate_b.shape[0], 1, 2 * W), _rows3(lam), w_out)


def _b_qkv_kernel(h_ref, g_ref, w_ref, cos_ref, sin_ref, *refs):
    outs, zs_ref = refs[:-1], refs[-1]
    G = len(DILATIONS)
    tm = h_ref.shape[0]
    hn = _rms(h_ref[...], g_ref[...]).astype(BF16)
    cos = cos_ref[...]
    sin = sin_ref[...]
    lane = lax.broadcasted_iota(jnp.int32, (1, LANES), 1)
    first_half = (lane % ATT_HEAD_DIM) < (ATT_HEAD_DIM // 2)
    blk = outs[0].shape[-1]

    def emit(z, dst_ref, d):
        if d == 1:
            dst_ref[0] = z.astype(dst_ref.dtype)
            return
        for c in range(blk // LANES):
            zs_ref[c] = z[:, LANES * c:LANES * (c + 1)]
        for r in range(d):
            for c in range(blk // LANES):
                part = zs_ref.at[c][pl.ds(r, tm // d, stride=d), :]
                dst_ref[r, :, LANES * c:LANES * (c + 1)] = part.astype(dst_ref.dtype)

    for j in range(2 * G + 1):
        z = _dot(hn, w_ref[:, blk * j:blk * (j + 1)])
        if j < 2 * G:
            cols = []
            for c in range(blk // LANES):
                zc = z[:, LANES * c:LANES * (c + 1)]
                up = pltpu.roll(zc, LANES - ATT_HEAD_DIM // 2, axis=1)
                dn = pltpu.roll(zc, ATT_HEAD_DIM // 2, axis=1)
                oc = zc * cos + jnp.where(first_half, up, dn) * sin
                if j < G:
                    oc = oc * (ATT_HEAD_DIM ** -0.5)
                cols.append(oc)
            z = jnp.concatenate(cols, axis=1)
            emit(z, outs[j], DILATIONS[j % G])
        else:
            for g in range(G):
                emit(z, outs[2 * G + g], DILATIONS[g])


def _b_attn_kernel(q_ref, kp_ref, kc_ref, vp_ref, vc_ref, o_ref, l_ref):
    nr, rows, width = q_ref.shape
    nb = rows // BAND
    first = pl.program_id(2) == 0
    row = lax.broadcasted_iota(jnp.int32, (BAND, 2 * BAND), 0)
    col = lax.broadcasted_iota(jnp.int32, (BAND, 2 * BAND), 1)
    dist = BAND + row - col
    band = (dist >= 0) & (dist <= BAND)
    band0 = band & ((col >= BAND) | jnp.logical_not(first))
    lane = lax.broadcasted_iota(jnp.int32, (1, LANES), 1)
    left = lane < ATT_HEAD_DIM
    zero = jnp.zeros((), BF16)
    npair = width // LANES
    pairs = [slice(LANES * p, LANES * (p + 1)) for p in range(npair)]
    q, kp, kc, vp, vc = q_ref[...], kp_ref[...], kc_ref[...], vp_ref[...], vc_ref[...]

    def keys(prev, cur, r, s, sl):
        if s == 0:
            return jnp.concatenate([prev[r, :, sl], cur[r, :BAND, sl]], axis=0)
        return cur[r, BAND * (s - 1):BAND * (s + 1), sl]

    blocks = [(r, s) for r in range(nr) for s in range(nb)]
    ks = {(r, s, p): keys(kp, kc, r, s, sl) for r, s in blocks for p, sl in enumerate(pairs)}
    vs = {(r, s, p): keys(vp, vc, r, s, sl) for r, s in blocks for p, sl in enumerate(pairs)}
    items = [(r, s, p, hh) for r, s in blocks for p in range(npair) for hh in range(2)]
    ss = [jnp.where(band0 if s == 0 else band,
                    _dot_nt(jnp.where(left if hh == 0 else ~left, q[r, BAND * s:BAND * (s + 1), pairs[p]], zero),
                            ks[r, s, p]), NEG_INF)
          for r, s, p, hh in items]
    ms = [jnp.max(x, axis=-1, keepdims=True) for x in ss]
    es = [jnp.exp(x - m) for x, m in zip(ss, ms)]
    dens = [jnp.sum(e, axis=-1, keepdims=True) for e in es]
    pvs = [_dot(e.astype(BF16), vs[r, s, p]) for e, (r, s, p, _) in zip(es, items)]
    outs = [pv / den for pv, den in zip(pvs, dens)]
    lses = [m + jnp.log(den) for m, den in zip(ms, dens)]

    def tile(parts, r):
        base = 2 * npair * nb * r
        rows_ = [jnp.concatenate([jnp.where(left, parts[base + 2 * (npair * s + p)], parts[base + 2 * (npair * s + p) + 1])
                                  for p in range(npair)], axis=1) for s in range(nb)]
        return jnp.concatenate(rows_, axis=0)

    for r in range(nr):
        o_ref[r] = tile(outs, r).astype(o_ref.dtype)
        l_ref[r] = tile(lses, r)


def _b_out_kernel(o1_ref, o2_ref, o3_ref, l1_ref, l2_ref, l3_ref, h_ref, wo_ref, out_ref, *scratch):
    def rows(ref, scr):
        d, n = ref.shape[0], ref.shape[1]
        if d == 1:
            return ref[0].astype(F32)
        ncol = ref.shape[2] // LANES
        for r in range(d):
            part = ref[r].astype(F32)
            for c in range(ncol):
                scr.at[c][pl.ds(r, n, stride=d), :] = part[:, LANES * c:LANES * (c + 1)]
        return jnp.concatenate([scr[c] for c in range(ncol)], axis=1)

    l1, l2, l3 = rows(l1_ref, None), rows(l2_ref, scratch[0]), rows(l3_ref, scratch[1])
    m = jnp.maximum(jnp.maximum(l1, l2), l3)
    e1, e2, e3 = jnp.exp(l1 - m), jnp.exp(l2 - m), jnp.exp(l3 - m)
    o = e1 * rows(o1_ref, None) + e2 * rows(o2_ref, scratch[2]) + e3 * rows(o3_ref, scratch[3])
    o = o / (e1 + e2 + e3)
    out_ref[...] = h_ref[...] + _dot(o.astype(BF16), wo_ref[...])


def _attention_layer(h, j, norm_g, cos_t, sin_t, w_qkv, w_out, *, batch, seq):
    T, D = h.shape
    HD = w_out.shape[1]
    NQ = w_qkv.shape[2]
    G = len(DILATIONS)
    tm = 512
    tps = seq // tm

    def split_spec(d):
        return pl.BlockSpec((None, d, tm // d, HD), lambda i: (i // tps, 0, i % tps, 0))

    def split_shape(d, dtype):
        return jax.ShapeDtypeStruct((batch, d, seq // d, HD), dtype)

    qkv = pl.pallas_call(
        _b_qkv_kernel,
        grid=(T // tm,),
        in_specs=[
            pl.BlockSpec((tm, D), lambda i: (i, 0)),
            _const_spec((1, D)),
            _layer_spec(j, D, NQ),
            pl.BlockSpec((tm, LANES), lambda i: (i, 0)),
            pl.BlockSpec((tm, LANES), lambda i: (i, 0)),
        ],
        out_specs=[split_spec(d) for _ in range(3) for d in DILATIONS],
        out_shape=[split_shape(d, BF16) for _ in range(3) for d in DILATIONS],
        scratch_shapes=[pltpu.VMEM((HD // LANES, tm, LANES), F32)],
        compiler_params=_params("parallel"),
        name="b_qkv",
    )(h, norm_g.reshape(1, D), w_qkv, cos_t, sin_t)

    outs, lses = [], []
    for g, d in enumerate(DILATIONS):
        q_g, k_g, v_g = qkv[g], qkv[G + g], qkv[2 * G + g]
        nblk = seq // d // BAND
        nb = min(4, nblk)
        nr = 4 // nb
        one = (None, nr, BAND, HD)
        many = (None, nr, nb * BAND, HD)
        cur = lambda b, r, m: (b, r, m, 0)
        prev = lambda b, r, m, nb=nb: (b, r, jnp.maximum(nb * m - 1, 0), 0)
        o_g, l_g = pl.pallas_call(
            _b_attn_kernel,
            grid=(batch, d // nr, nblk // nb),
            in_specs=[pl.BlockSpec(many, cur), pl.BlockSpec(one, prev), pl.BlockSpec(many, cur),
                      pl.BlockSpec(one, prev), pl.BlockSpec(many, cur)],
            out_specs=[pl.BlockSpec(many, cur), pl.BlockSpec(many, cur)],
            out_shape=[split_shape(d, BF16), split_shape(d, F32)],
            compiler_params=_params("parallel", "parallel", "arbitrary"),
            name=f"b_attn_d{d}",
        )(q_g, k_g, k_g, v_g, v_g)
        outs.append(o_g)
        lses.append(l_g)

    row = lambda i: (i, 0)
    return pl.pallas_call(
        _b_out_kernel,
        grid=(T // tm,),
        in_specs=[split_spec(d) for _ in range(2) for d in DILATIONS]
        + [pl.BlockSpec((tm, D), row), _layer_spec(j, HD, D)],
        out_specs=pl.BlockSpec((tm, D), row),
        out_shape=jax.ShapeDtypeStruct((T, D), F32),
        scratch_shapes=[pltpu.VMEM((HD // LANES, tm, LANES), F32)] * 4,
        compiler_params=_params("parallel"),
        name="b_out",
    )(*outs, *lses, h, w_out)


def _c_in_kernel(h_ref, hh_ref, g_ref, mu_ref, wr_ref, wk_ref, wv_ref, w1_ref, w2_ref, a1_ref, a2_ref,
                 g1_ref, g2_ref, w0_ref, a0_ref, kk_ref, ka_ref, rk_ref,
                 rt_out, bh_out, gam_out, g_out, bonus_out, w_out, uv_out, yv_out, prb_out, vk_out,
                 *, tiles_per_seq):
    g = g_ref[...]
    hn = _rms(h_ref[...], g)
    halo = _rms(hh_ref[...], g)
    halo = jnp.where(pl.program_id(0) % tiles_per_seq == 0, 0.0, halo)
    he = jnp.concatenate([halo, hn], axis=0)
    xx = pltpu.roll(he, 1, axis=0)[HALO_F32:] - hn
    mu = mu_ref[...]

    def mix(c):
        return (hn + xx * mu[c:c + 1]).astype(BF16)

    r = _dot(mix(0), wr_ref[...])
    k = _dot(mix(1), wk_ref[...])
    v = _dot(mix(2), wv_ref[...])
    wl = _dot(jnp.tanh(_dot(mix(3), w1_ref[...])).astype(BF16), w2_ref[...])
    al = _dot(_dot(mix(4), a1_ref[...]).astype(BF16), a2_ref[...])
    g_out[...] = _dot(jax.nn.sigmoid(_dot(mix(5), g1_ref[...])).astype(BF16), g2_ref[...])

    z = -(w0_ref[...] + wl)
    w = -(jnp.maximum(z, 0.0) + jnp.log(1.0 + jnp.exp(-jnp.abs(z)))) - 0.5
    lw = -jnp.exp(w)
    a = jax.nn.sigmoid(a0_ref[...] + al)
    kk = k * kk_ref[...]
    kk = kk / jnp.maximum(jnp.sqrt(_seg_sum(kk * kk)), 1e-12)
    kh = k * (1.0 + (a - 1.0) * ka_ref[...])
    bb = kk * a
    bonus_out[...] = _seg_sum(r * kh * rk_ref[...]) * v

    tm = lw.shape[0]
    rowc = lax.broadcasted_iota(jnp.int32, (tm, 1), 0) % CHUNK
    cum = lw
    s = 1
    while s < CHUNK:
        cum = cum + jnp.where(rowc >= s, pltpu.roll(cum, s, axis=0), 0.0)
        s *= 2
    lasts = [jnp.broadcast_to(cum[CHUNK * (c + 1) - 1:CHUNK * (c + 1), :], (CHUNK, cum.shape[1]))
             for c in range(tm // CHUNK)]
    last = jnp.concatenate(lasts, axis=0)
    e_neg = jnp.exp(-cum)
    e_end = jnp.exp(last - cum)
    at = (-kk * jnp.exp(cum - lw)).astype(BF16)
    bt = (bb * e_neg).astype(BF16)
    kt = (kh * e_neg).astype(BF16)
    rt = (r * jnp.exp(cum)).astype(BF16)
    rt_out[...] = rt
    bh_out[...] = (bb * e_end).astype(bh_out.dtype)
    for c in range(tm // CHUNK):
        gam_out[HALO_F32 * c:HALO_F32 * (c + 1), :] = jnp.exp(lasts[c][:HALO_F32])
    _chunk_algebra(at, bt, kt, rt, (kh * e_end).astype(BF16), v.astype(BF16), w_out, uv_out, yv_out, prb_out, vk_out)


def _head_masks(shape):
    head = lax.broadcasted_iota(jnp.int32, shape, len(shape) - 1) // HEAD
    return [head == h for h in range(GROUP // HEAD)]


def _group_diag(x, dtype=BF16):
    x = x.astype(dtype)
    zero = jnp.zeros((), dtype)
    return jnp.concatenate([jnp.where(m, x, zero) for m in _head_masks((1, GROUP))], axis=0)


def _group_tri_inv(ms):
    c = ms[0].shape[0]
    row = lax.broadcasted_iota(jnp.int32, ms[0].shape, 0)
    col = lax.broadcasted_iota(jnp.int32, ms[0].shape, 1) % HEAD
    eye = jnp.where(row == col, 1.0, 0.0)
    first = (row >> 1 == col >> 1) & (row > col)
    xs = [eye + jnp.where(first, m, 0.0) for m in ms]
    mbs = [m.astype(BF16) for m in ms]
    heads = _head_masks(ms[0].shape)
    zero = jnp.zeros((), BF16)
    k = 1
    while (1 << k) < c:
        off = (row >> (k + 1) == col >> (k + 1)) & (((row >> k) & 1) == 1) & (((col >> k) & 1) == 0)
        offs = [off & hm for hm in heads]
        zs = [_dot(x.astype(BF16), jnp.concatenate([jnp.where(o, mb, zero) for o in offs], axis=0))
              for x, mb in zip(xs, mbs)]
        xs = [x + _dot(z.astype(BF16), _group_diag(x)) for x, z in zip(xs, zs)]
        k += 1
    return xs


def _chunk_algebra(at, bt, kt, rt, kh, v, w_out, uv_out, yv_out, prb_out, vk_out):
    C = CHUNK
    row = lax.broadcasted_iota(jnp.int32, (C, GROUP), 0)
    col = lax.broadcasted_iota(jnp.int32, (C, GROUP), 1) % HEAD
    strict = row > col
    incl = row >= col
    heads = _head_masks((1, GROUP))
    zero = jnp.zeros((), BF16)
    ngrp = at.shape[1] // GROUP
    nchunk = at.shape[0] // C
    items = [(slice(C * c, C * (c + 1)), slice(GROUP * p, GROUP * (p + 1))) for c in range(nchunk) for p in range(ngrp)]
    bigs = []
    for rs, sl in items:
        lhs = jnp.concatenate([at[rs, sl], rt[rs, sl]], axis=0)
        rhs = jnp.concatenate([jnp.where(m, bt[rs, sl], zero) for m in heads]
                              + [jnp.where(m, kt[rs, sl], zero) for m in heads], axis=0)
        bigs.append(_dot_nt(lhs, rhs))
    v_bds = [_group_diag(v[rs, sl]) for rs, sl in items]
    gs = [_dot(jnp.where(strict, big[:C, GROUP:], 0.0).astype(BF16), v_bd) for big, v_bd in zip(bigs, v_bds)]
    yvs = [_dot(jnp.where(incl, big[C:, GROUP:], 0.0).astype(BF16), v_bd) for big, v_bd in zip(bigs, v_bds)]
    prbs = [jnp.where(incl, big[C:, :GROUP], 0.0).astype(prb_out.dtype) for big in bigs]
    vks = []
    for rs, sl in items:
        full = _dot(v[rs, sl].astype(F32).T.astype(BF16), kh[rs, sl])
        vks.append(sum(jnp.where(m, full[HEAD * h:HEAD * (h + 1)], 0.0) for h, m in enumerate(heads)))
    t_invs = [t.astype(BF16) for t in _group_tri_inv([jnp.where(strict, big[:C, :GROUP], 0.0) for big in bigs])]
    ws = [_dot(t, _group_diag(at[rs, sl])).astype(w_out.dtype) for t, (rs, sl) in zip(t_invs, items)]
    uvs = [_dot(t, _group_diag(g)) for t, g in zip(t_invs, gs)]

    def tile(parts):
        return jnp.concatenate(
            [jnp.concatenate(parts[ngrp * c:ngrp * (c + 1)], axis=1) for c in range(nchunk)], axis=0)

    yv_out[...] = tile(yvs)
    prb_out[...] = tile(prbs)
    w_out[...] = tile(ws)
    uv_out[...] = tile(uvs)
    vk_out[...] = tile(vks)


def _c_state_kernel(w_ref, uv_ref, rt_ref, yv_ref, bh_ref, prb_ref, vk_ref, gam_ref, bonus_ref, g_ref, h_ref,
                    lnw_ref, lnb_ref, wo_ref, out_ref, s_ref):
    @pl.when(pl.program_id(0) == 0)
    def _():
        s_ref[...] = jnp.zeros_like(s_ref)

    nb, C, D = w_ref.shape
    prow = lax.broadcasted_iota(jnp.int32, (GROUP, GROUP), 0) // HEAD
    pcol = lax.broadcasted_iota(jnp.int32, (GROUP, GROUP), 1) // HEAD
    same_head = prow == pcol
    ngrp = D // GROUP
    items = [(b, p, slice(GROUP * p, GROUP * (p + 1))) for b in range(nb) for p in range(ngrp)]
    w, uv, rt, yv, bh, prb, vk = (w_ref[...], uv_ref[...], rt_ref[...], yv_ref[...], bh_ref[...], prb_ref[...],
                                  vk_ref[...])
    states = [s_ref[b * ngrp + p] for b, p, _ in items]
    urs = [_dot_nt(jnp.concatenate([w[b, :, sl], rt[b, :, sl]], axis=0), st.astype(BF16))
           for (b, _, sl), st in zip(items, states)]
    us = [ur[:C] + uv[b, :, sl] for ur, (b, _, sl) in zip(urs, items)]
    dss = [_dot(u.T.astype(BF16), bh[b, :, sl]) for u, (b, _, sl) in zip(us, items)]
    for (b, p, sl), st, ds in zip(items, states, dss):
        s_ref[b * ngrp + p] = (st * gam_ref[b, 0:1, sl] + jnp.where(same_head, ds, 0.0)
                               + _group_diag(vk[b, :, sl], F32))
    pus = [_dot(prb[b, :, sl], _group_diag(u)) for u, (b, _, sl) in zip(us, items)]
    ys = [ur[C:] + pu + yv[b, :, sl] for ur, pu, (b, _, sl) in zip(urs, pus, items)]

    y = jnp.concatenate([jnp.concatenate(ys[ngrp * b:ngrp * (b + 1)], axis=1) for b in range(nb)], axis=0)
    mean = _seg_sum(y) * (1.0 / HEAD)
    yc = y - mean
    var = _seg_sum(yc * yc) * (1.0 / HEAD)
    yn = (yc * lax.rsqrt(var + GN_EPS)) * lnw_ref[...] + lnb_ref[...]
    gated = ((yn + bonus_ref[...].reshape(nb * C, D)) * g_ref[...].reshape(nb * C, D)).astype(BF16)
    out_ref[...] = h_ref[...] + _dot(gated, wo_ref[...]).reshape(nb, C, D)


def _rwkv_layer(h, j, norm_g, mu, w_rkv, w0, w1, w2, a0, a1, a2, g1, g2, k_k, k_a, r_k, ln_w, ln_b, w_out,
                *, batch, seq):
    T, D = h.shape
    C = CHUNK
    tm = 256
    nc = T // C
    row = lambda i: (i, 0)
    halo_spec = pl.BlockSpec((HALO_F32, D), lambda i: (jnp.maximum(i * (tm // HALO_F32) - 1, 0), 0))
    vspec = _layer_spec(j, 1, D)
    mat_spec = lambda c: pl.BlockSpec((None, None, D, D), lambda *_: (j, c, 0, 0), pipeline_mode=pl.Buffered(1))
    consts = [norm_g.reshape(1, D), mu, w_rkv, w_rkv, w_rkv, w1, w2, a1, a2, g1, g2,
              _rows3(w0), _rows3(a0), _rows3(k_k), _rows3(k_a), r_k.reshape(r_k.shape[0], 1, D)]
    const_specs = [_const_spec((1, D)), _layer_spec(j, mu.shape[1], D), mat_spec(0), mat_spec(1), mat_spec(2),
                   _layer_spec(j, D, w1.shape[2]), _layer_spec(j, w2.shape[1], D),
                   _layer_spec(j, D, a1.shape[2]), _layer_spec(j, a2.shape[1], D),
                   _layer_spec(j, D, g1.shape[2]), _layer_spec(j, g2.shape[1], D),
                   vspec, vspec, vspec, vspec, vspec]
    tile = pl.BlockSpec((tm, D), row)
    tile_shape = lambda dt: jax.ShapeDtypeStruct((T, D), dt)
    rt, bh, gam, gg, bonus, wq, uv, yv, prb, vk = pl.pallas_call(
        functools.partial(_c_in_kernel, tiles_per_seq=seq // tm),
        grid=(T // tm,),
        in_specs=[tile, halo_spec] + const_specs,
        out_specs=[tile, tile, pl.BlockSpec((tm // C * HALO_F32, D), row), tile, tile, tile, tile, tile, tile, tile],
        out_shape=[tile_shape(BF16), tile_shape(BF16), jax.ShapeDtypeStruct((nc * HALO_F32, D), F32),
                   tile_shape(F32), tile_shape(F32), tile_shape(BF16), tile_shape(F32), tile_shape(F32),
                   tile_shape(BF16), tile_shape(F32)],
        compiler_params=_params("parallel"),
        name="c_in",
    )(h, h, *consts)

    cps = seq // C
    per_seq = lambda x: x.reshape(batch, x.shape[0] // batch, x.shape[1])
    cblk = lambda rows, cols: pl.BlockSpec((batch, rows, cols), lambda c: (0, c, 0))
    return pl.pallas_call(
        _c_state_kernel,
        grid=(cps,),
        in_specs=[cblk(C, D)] * 7 + [cblk(HALO_F32, D)] + [cblk(C, D)] * 3
        + [vspec, vspec, _layer_spec(j, D, D)],
        out_specs=cblk(C, D),
        out_shape=jax.ShapeDtypeStruct((batch, seq, D), F32),
        scratch_shapes=[pltpu.VMEM((batch * (D // GROUP), GROUP, GROUP), F32)],
        compiler_params=_params("arbitrary"),
        name="c_state",
    )(*[per_seq(x) for x in (wq, uv, rt, yv, bh, prb, vk, gam, bonus, gg, h)],
      _rows3(ln_w), _rows3(ln_b), w_out).reshape(T, D)


def _ffn_ple_kernel(h_ref, hh_ref, p_ref, g_ref, wg_ref, wu_ref, cwg_ref, cwu_ref, cbg_ref, cbu_ref, wd_ref,
                    gp_ref, wpg_ref, wpp_ref, gf_ref, out_ref, hn_ref, act_ref, *, tiles_per_seq, fc, final_norm):
    g = g_ref[...]
    h = h_ref[...]
    halo = _rms(hh_ref[...], g)
    halo = jnp.where(pl.program_id(0) % tiles_per_seq == 0, 0.0, halo)
    hn_ref[:HALO_BF16, :] = halo.astype(BF16)
    hn_ref[HALO_BF16:, :] = _rms(h, g).astype(BF16)

    def conv(u, cw, cb):
        y = cb + cw[2:3] * u + cw[1:2] * pltpu.roll(u, 1, axis=0) + cw[0:1] * pltpu.roll(u, 2, axis=0)
        return y[HALO_BF16:]

    for c in range(act_ref.shape[1] // fc):
        sl = slice(fc * c, fc * (c + 1))
        hn = hn_ref[...]
        yg = conv(_dot(hn, wg_ref[:, sl]), cwg_ref[:, sl], cbg_ref[:, sl])
        yu = conv(_dot(hn, wu_ref[:, sl]), cwu_ref[:, sl], cbu_ref[:, sl])
        hg = 0.5 * yg
        act_ref[:, sl] = ((hg + hg * jnp.tanh(hg)) * yu).astype(BF16)
    h = h + _dot(act_ref[...], wd_ref[...])

    gate = jax.nn.sigmoid(_dot(_rms(h, gp_ref[...]).astype(BF16), wpg_ref[...]))
    h = h + gate * _dot(p_ref[...].astype(BF16), wpp_ref[...])
    if final_norm:
        h = _rms(h, gf_ref[...])
    out_ref[...] = h


def _ffn_ple_layer(h, p, layer, norm_ffn, w_up, conv_w, conv_b, w_down, norm_ple, w_gate, w_proj, norm_final,
                   *, seq, final_norm):
    T, D = h.shape
    F = w_down.shape[1]
    P = p.shape[-1]
    fc = 256
    tm = 512
    halo_spec = pl.BlockSpec((HALO_BF16, D), lambda i: (jnp.maximum(i * (tm // HALO_BF16) - 1, 0), 0))
    kc = conv_w.shape[1]
    return pl.pallas_call(
        functools.partial(_ffn_ple_kernel, tiles_per_seq=seq // tm, fc=fc, final_norm=final_norm),
        grid=(T // tm,),
        in_specs=[
            pl.BlockSpec((tm, D), lambda i: (i, 0)), halo_spec,
            pl.BlockSpec((None, tm, P), lambda i: (layer, i, 0)),
            _layer_spec(layer, 1, D),
            _layer_spec(layer, D, F, 0), _layer_spec(layer, D, F, 1),
            _layer_spec(layer, kc, F, 0), _layer_spec(layer, kc, F, 1),
            _layer_spec(layer, 1, F, 0), _layer_spec(layer, 1, F, 1),
            _layer_spec(layer, F, D),
            _layer_spec(layer, 1, D), _layer_spec(layer, D, D), _layer_spec(layer, P, D), _const_spec((1, D)),
        ],
        out_specs=pl.BlockSpec((tm, D), lambda i: (i, 0)),
        out_shape=jax.ShapeDtypeStruct((T, D), F32),
        scratch_shapes=[pltpu.VMEM((tm + HALO_BF16, D), BF16), pltpu.VMEM((tm, F), BF16)],
        compiler_params=_params("parallel"),
        name="ffn_ple",
    )(h, h, p, _rows3(norm_ffn), w_up, w_up, conv_w, conv_w, _rows3(conv_b), _rows3(conv_b), w_down,
      _rows3(norm_ple), w_gate, w_proj, norm_final.reshape(1, D))


def _rotary_tables(positions):
    inv_freq = ROPE_THETA ** (-jnp.arange(0, ATT_HEAD_DIM, 2, dtype=F32) / ATT_HEAD_DIM)
    ang = positions.astype(F32).reshape(-1, 1) * inv_freq
    cos, sin = jnp.cos(ang), jnp.sin(ang)
    reps = LANES // ATT_HEAD_DIM
    cos_t = jnp.tile(jnp.concatenate([cos, cos], axis=1), (1, reps))
    sin_t = jnp.tile(jnp.concatenate([-sin, sin], axis=1), (1, reps))
    return cos_t, sin_t


def kernel(x, p, positions, norm_mix, norm_ffn, norm_ple, norm_final, a_w_in, a_conv_w, a_conv_b, a_gate_w, a_gate_b, a_lambda, a_w_out, b_w_qkv, b_w_out, c_mu, c_w_rkv, c_w0, c_w1, c_w2, c_a0, c_a1, c_a2, c_g1, c_g2, c_k_k, c_k_a, c_r_k, c_ln_w, c_ln_b, c_w_out, f_w_up, f_conv_w, f_conv_b, f_w_down, ple_w_proj, ple_w_gate):
    batch, seq, D = x.shape
    depth = norm_mix.shape[0]
    T = batch * seq
    h = x.reshape(T, D)
    pf = p.reshape(depth, T, p.shape[-1])
    cos_t, sin_t = _rotary_tables(positions)
    bf = lambda w: w.astype(BF16)
    a_w_in, a_gate_w, a_w_out, b_w_qkv, b_w_out = bf(a_w_in), bf(a_gate_w), bf(a_w_out), bf(b_w_qkv), bf(b_w_out)
    c_w_rkv, c_w1, c_w2, c_a1, c_a2, c_g1, c_g2, c_w_out = (bf(c_w_rkv), bf(c_w1), bf(c_w2), bf(c_a1), bf(c_a2),
                                                           bf(c_g1), bf(c_g2), bf(c_w_out))
    f_w_up, f_w_down, ple_w_gate, ple_w_proj = bf(f_w_up), bf(f_w_down), bf(ple_w_gate), bf(ple_w_proj)
    for i in range(depth):
        kind, j = i % 3, i // 3
        if kind == 0:
            h = _rglru_layer(h, j, norm_mix[i], a_w_in, a_conv_w, a_conv_b, a_gate_w, a_gate_b, a_lambda, a_w_out,
                             batch=batch, seq=seq)
        elif kind == 1:
            h = _attention_layer(h, j, norm_mix[i], cos_t, sin_t, b_w_qkv, b_w_out, batch=batch, seq=seq)
        else:
            h = _rwkv_layer(h, j, norm_mix[i], c_mu, c_w_rkv, c_w0, c_w1, c_w2, c_a0, c_a1, c_a2, c_g1, c_g2,
                            c_k_k, c_k_a, c_r_k, c_ln_w, c_ln_b, c_w_out, batch=batch, seq=seq)
        h = _ffn_ple_layer(h, pf, i, norm_ffn, f_w_up, f_conv_w, f_conv_b, f_w_down, norm_ple, ple_w_gate,
                           ple_w_proj, norm_final, seq=seq, final_norm=(i == depth - 1))
    return h.reshape(batch, seq, D)
```
